```python
import jax
import jax.numpy as jnp
from jax import lax
import numpy as np

D_MODEL = 1024
BATCH = 8
SEQ = 4096
DEPTH = 2

NORM_EPS = 1e-6
MASK_VALUE = -1e30
MIN_GATE = 1e-30
HG_HEADS = 8
HG_DK = 128
HG_DV = 128
HG_WIDTH = HG_HEADS * HG_DK
HG_CHUNK = 32
RW_HEAD_SIZE = 64
RW_HEADS = 16
RW_WIDTH = RW_HEADS * RW_HEAD_SIZE
RW_LORA_W = 64
RW_LORA_A = 64
RW_LORA_G = 128
RW_LORA_V = 32
RW_LNX_EPS = 1e-5 * RW_HEAD_SIZE
RW_IN = 3 * RW_WIDTH + RW_LORA_W + RW_LORA_A + RW_LORA_G
MLA_HEADS = 8
MLA_Q_RANK = 384
MLA_KV_RANK = 256
MLA_NOPE = 128
MLA_ROPE = 64
MLA_V = 128
MLA_WIDTH = MLA_HEADS * MLA_V
ROPE_THETA = 10000.0
Q_BLOCK = 128
N_BRANCH = 3
D_FF = ((-(-8 * D_MODEL // 3) + 255) // 256) * 256
IN_SIZES = (HG_WIDTH, HG_WIDTH, HG_WIDTH, HG_WIDTH, RW_IN, MLA_Q_RANK, MLA_KV_RANK, MLA_ROPE, N_BRANCH * D_MODEL)
IN_WIDTH = sum(IN_SIZES)
IN_SPLITS = tuple(int(c) for c in np.cumsum(IN_SIZES))
RW_SPLITS = tuple(int(c) for c in np.cumsum((RW_WIDTH, RW_WIDTH, RW_WIDTH, RW_LORA_W, RW_LORA_A)))

kernel_name = 'hybrid_hgrn2_rwkv7_mla_gated_block'


def rms_norm(x, g):
    xf = x.astype(jnp.float32)
    y = xf * lax.rsqrt(jnp.mean(xf * xf, axis=-1, keepdims=True) + NORM_EPS)
    return (y * g.astype(jnp.float32)).astype(x.dtype)


def token_shift(z, mu):
    z_prev = jnp.pad(z, ((0, 0), (1, 0), (0, 0)))[:, :-1]
    return z + (z_prev - z) * mu


def rope_angles(positions):
    inv_freq = ROPE_THETA ** (-jnp.arange(0, MLA_ROPE, 2, dtype=jnp.float32) / MLA_ROPE)
    ang = positions.astype(jnp.float32)[..., None] * inv_freq
    return jnp.cos(ang), jnp.sin(ang)


def apply_rope(x, cos, sin):
    x1, x2 = jnp.split(x, 2, axis=-1)
    return jnp.concatenate([x1 * cos - x2 * sin, x1 * sin + x2 * cos], axis=-1).astype(x.dtype)


def hgrn2_mixer(zq, zf, zi, zog, lower_bound, onorm_g):
    B, S, _ = zq.shape
    nc = S // HG_CHUNK
    zf32 = zf.astype(jnp.float32)
    f = lower_bound + (1.0 - lower_bound) * jax.nn.sigmoid(zf32)
    log_f = jnp.log(jnp.maximum(f, MIN_GATE))
    k = (1.0 - lower_bound) * jax.nn.sigmoid(-zf32)

    def chunks(t, d):
        return t.astype(jnp.float32).reshape(B, nc, HG_CHUNK, HG_HEADS, d).transpose(1, 0, 3, 2, 4)

    causal = jnp.tril(jnp.ones((HG_CHUNK, HG_CHUNK), dtype=bool))[:, :, None]

    def chunk_step(state, inp):
        q, kc, v, lf = inp
        b = jnp.cumsum(lf, axis=2)
        diff = b[:, :, :, None, :] - b[:, :, None, :, :]
        rel = jnp.where(causal, jnp.exp(jnp.where(causal, diff, 0.0)), 0.0)
        scores = jnp.einsum('bhtd,bhsd,bhtsd->bhts', q, kc, rel)
        o = jnp.einsum('bhts,bhsv->bhtv', scores, v) + jnp.einsum('bhtd,bhdv->bhtv', q * jnp.exp(b), state)
        b_last = b[:, :, -1:, :]
        state = jnp.exp(b_last[:, :, 0, :, None]) * state + jnp.einsum('bhsd,bhsv->bhdv', kc * jnp.exp(b_last - b), v)
        return state, o

    state0 = jnp.zeros((B, HG_HEADS, HG_DK, HG_DV), jnp.float32)
    _, o = lax.scan(chunk_step, state0, (chunks(zq, HG_DK), chunks(k, HG_DK), chunks(zi, HG_DV), chunks(log_f, HG_DK)))
    o = o.transpose(1, 0, 3, 2, 4).reshape(B, S, HG_HEADS, HG_DV)
    o = rms_norm(o, onorm_g).reshape(B, S, HG_HEADS * HG_DV)
    return (o * jax.nn.silu(zog.astype(jnp.float32))).astype(zq.dtype)


def rwkv7_mixer(zrw, mu, w0, w_up, a0, a_up, g_up, k_k, k_a, r_k, lnx_g, lnx_b, v_first, vres):
    B, S, _ = zrw.shape
    zs = token_shift(zrw, mu).astype(jnp.float32)
    r, k, v, zw, za, zg = jnp.split(zs, RW_SPLITS, axis=-1)
    w_log = -jax.nn.softplus(-(w0 + jnp.tanh(zw) @ w_up)) - 0.5
    decay = jnp.exp(-jnp.exp(w_log))
    a = jax.nn.sigmoid(a0 + za @ a_up)
    g = jax.nn.sigmoid(zg) @ g_up
    if vres is None:
        v_first = v
    else:
        zv, v0, v_up = vres
        v = v + (v_first - v) * jax.nn.sigmoid(v0 + zv @ v_up)

    def heads(t):
        return t.reshape(B, S, RW_HEADS, RW_HEAD_SIZE)

    kk = heads(k * k_k)
    kk = kk * lax.rsqrt(jnp.maximum(jnp.sum(kk * kk, axis=-1, keepdims=True), 1e-24))
    k = k * (1.0 + (a - 1.0) * k_a)
    rh, kh, vh, wh, ah = heads(r), heads(k), heads(v), heads(decay), heads(a)

    def step(state, inp):
        r_t, w_t, k_t, v_t, kk_t, a_t = inp
        sa = jnp.einsum('bhvk,bhk->bhv', state, -kk_t)
        state = (state * w_t[:, :, None, :] + sa[..., None] * (kk_t * a_t)[:, :, None, :]
                 + v_t[..., None] * k_t[:, :, None, :])
        return state, jnp.einsum('bhvk,bhk->bhv', state, r_t)

    state0 = jnp.zeros((B, RW_HEADS, RW_HEAD_SIZE, RW_HEAD_SIZE), jnp.float32)
    xs = tuple(t.transpose(1, 0, 2, 3) for t in (rh, wh, kh, vh, kk, ah))
    _, y = lax.scan(step, state0, xs)
    y = y.transpose(1, 0, 2, 3)
    mean = jnp.mean(y, axis=-1, keepdims=True)
    var = jnp.mean(jnp.square(y - mean), axis=-1, keepdims=True)
    y = ((y - mean) * lax.rsqrt(var + RW_LNX_EPS)).reshape(B, S, RW_WIDTH) * lnx_g + lnx_b
    bonus = jnp.sum(rh * kh * r_k.reshape(RW_HEADS, RW_HEAD_SIZE), axis=-1, keepdims=True) * vh
    y = (y + bonus.reshape(B, S, RW_WIDTH)) * g
    return y.astype(zrw.dtype), v_first


def mla_mixer(zcq, zckv, zkr, cos, sin, q_norm_g, w_uq, kv_norm_g, w_ukv):
    B, S, _ = zcq.shape
    q = (rms_norm(zcq, q_norm_g) @ w_uq).reshape(B, S, MLA_HEADS, MLA_NOPE + MLA_ROPE)
    q_nope = q[..., :MLA_NOPE]
    q_rope = apply_rope(q[..., MLA_NOPE:], cos[:, :, None, :], sin[:, :, None, :])
    kv = (rms_norm(zckv, kv_norm_g) @ w_ukv).reshape(B, S, MLA_HEADS, MLA_NOPE + MLA_V)
    k_nope = kv[..., :MLA_NOPE].transpose(0, 2, 1, 3)
    v = kv[..., MLA_NOPE:].transpose(0, 2, 1, 3)
    k_rope = apply_rope(zkr, cos, sin)
    nb = S // Q_BLOCK

    def q_blocks(t):
        return t.reshape(B, nb, Q_BLOCK, MLA_HEADS, t.shape[-1]).transpose(1, 0, 3, 2, 4)

    scale = (MLA_NOPE + MLA_ROPE) ** -0.5
    key_pos = jnp.arange(S)

    def attend_block(inp):
        qn, qr, blk = inp
        s = jnp.einsum('bhqd,bhkd->bhqk', qn, k_nope) + jnp.einsum('bhqd,bkd->bhqk', qr, k_rope)
        s = s.astype(jnp.float32) * scale
        q_pos = blk * Q_BLOCK + jnp.arange(Q_BLOCK)
        s = jnp.where(key_pos[None, :] <= q_pos[:, None], s, MASK_VALUE)
        p = jax.nn.softmax(s, axis=-1).astype(v.dtype)
        return jnp.einsum('bhqk,bhkd->bhqd', p, v)

    o = lax.map(attend_block, (q_blocks(q_nope), q_blocks(q_rope), jnp.arange(nb)))
    return o.transpose(1, 0, 3, 2, 4).reshape(B, S, MLA_WIDTH)


def setup_inputs(seed: int = 0) -> dict:
    key = jax.random.key(seed)
    ks = iter(jax.random.split(key, 48))
    L = DEPTH
    LV = DEPTH - 1
    D = D_MODEL

    def nrm(shape, scale):
        return jax.random.normal(next(ks), shape, jnp.float32) * scale

    def gain(shape):
        return 1.0 + nrm(shape, 0.05)

    x = nrm((BATCH, SEQ, D), 1.0)
    offsets = jax.random.randint(next(ks), (BATCH, 1), 0, 1024, dtype=jnp.int32)
    positions = offsets + jnp.arange(SEQ, dtype=jnp.int32)[None, :]
    return {
        'x': x,
        'positions': positions,
        'hgrn_lb_logits': nrm((L, HG_WIDTH), 0.5),
        'mix_pre_g': gain((L, D)),
        'mix_post_g': gain((L, D)),
        'ffn_pre_g': gain((L, D)),
        'ffn_post_g': gain((L, D)),
        'w_in': nrm((L, D, IN_WIDTH), D ** -0.5),
        'w_vres_down': nrm((LV, D, RW_LORA_V), D ** -0.5),
        'hgrn_onorm_g': gain((L, HG_DV)),
        'rwkv_mu': jax.random.uniform(next(ks), (L, RW_IN), jnp.float32, 0.1, 0.9),
        'rwkv_vres_mu': jax.random.uniform(next(ks), (LV, RW_LORA_V), jnp.float32, 0.1, 0.9),
        'rwkv_w0': nrm((L, RW_WIDTH), 0.5),
        'rwkv_w_up': nrm((L, RW_LORA_W, RW_WIDTH), 0.1),
        'rwkv_a0': nrm((L, RW_WIDTH), 0.2),
        'rwkv_a_up': nrm((L, RW_LORA_A, RW_WIDTH), 0.5 * RW_LORA_A ** -0.5),
        'rwkv_g_up': nrm((L, RW_LORA_G, RW_WIDTH), RW_LORA_G ** -0.5),
        'rwkv_v0': nrm((LV, RW_WIDTH), 0.2),
        'rwkv_v_up': nrm((LV, RW_LORA_V, RW_WIDTH), 0.5 * RW_LORA_V ** -0.5),
        'rwkv_k_k': 0.85 + nrm((L, RW_WIDTH), 0.05),
        'rwkv_k_a': 1.0 + nrm((L, RW_WIDTH), 0.05),
        'rwkv_r_k': nrm((L, RW_WIDTH), 0.1),
        'rwkv_lnx_g': gain((L, RW_WIDTH)),
        'rwkv_lnx_b': nrm((L, RW_WIDTH), 0.02),
        'mla_q_norm_g': gain((L, MLA_Q_RANK)),
        'mla_w_uq': nrm((L, MLA_Q_RANK, MLA_HEADS * (MLA_NOPE + MLA_ROPE)), MLA_Q_RANK ** -0.5),
        'mla_kv_norm_g': gain((L, MLA_KV_RANK)),
        'mla_w_ukv': nrm((L, MLA_KV_RANK, MLA_HEADS * (MLA_NOPE + MLA_V)), MLA_KV_RANK ** -0.5),
        'w_branch': nrm((L, N_BRANCH, HG_WIDTH, D), HG_WIDTH ** -0.5),
        'w_out': nrm((L, D, D), D ** -0.5),
        'w_ffn_in': nrm((L, D, 2 * D_FF), D ** -0.5),
        'w_ffn_out': nrm((L, D_FF, D), D_FF ** -0.5),
    }


def reference(x, positions, hgrn_lb_logits, mix_pre_g, mix_post_g, ffn_pre_g, ffn_post_g, w_in, w_vres_down,
              hgrn_onorm_g, rwkv_mu, rwkv_vres_mu, rwkv_w0, rwkv_w_up, rwkv_a0, rwkv_a_up, rwkv_g_up, rwkv_v0,
              rwkv_v_up, rwkv_k_k, rwkv_k_a, rwkv_r_k, rwkv_lnx_g, rwkv_lnx_b, mla_q_norm_g, mla_w_uq,
              mla_kv_norm_g, mla_w_ukv, w_branch, w_out, w_ffn_in, w_ffn_out):
    probs = jax.nn.softmax(hgrn_lb_logits.astype(jnp.float32), axis=0)
    lower_bounds = jnp.cumsum(probs, axis=0) - probs[0]
    cos, sin = rope_angles(positions)
    v_first = None
    for l in range(DEPTH):
        h = rms_norm(x, mix_pre_g[l])
        w_cols = w_in[l] if l == 0 else jnp.concatenate([w_in[l], w_vres_down[l - 1]], axis=1)
        z = h @ w_cols
        zq, zf, zi, zog, zrw, zcq, zckv, zkr, zgate, zvres = jnp.split(z, IN_SPLITS, axis=-1)

        o_hg = hgrn2_mixer(zq, zf, zi, zog, lower_bounds[l], hgrn_onorm_g[l])
        vres = None if l == 0 else (token_shift(zvres, rwkv_vres_mu[l - 1]), rwkv_v0[l - 1], rwkv_v_up[l - 1])
        o_rw, v_first = rwkv7_mixer(zrw, rwkv_mu[l], rwkv_w0[l], rwkv_w_up[l], rwkv_a0[l], rwkv_a_up[l],
                                    rwkv_g_up[l], rwkv_k_k[l], rwkv_k_a[l], rwkv_r_k[l], rwkv_lnx_g[l],
                                    rwkv_lnx_b[l], v_first, vres)
        o_mla = mla_mixer(zcq, zckv, zkr, cos, sin, mla_q_norm_g[l], mla_w_uq[l], mla_kv_norm_g[l], mla_w_ukv[l])

        g_hg, g_rw, g_mla = jnp.split(jax.nn.sigmoid(zgate), N_BRANCH, axis=-1)
        merged = (g_hg * (o_hg @ w_branch[l, 0]) + g_rw * (o_rw @ w_branch[l, 1])
                  + g_mla * (o_mla @ w_branch[l, 2]))
        x = x + rms_norm(merged @ w_out[l], mix_post_g[l])

        h = rms_norm(x, ffn_pre_g[l])
        gate, up = jnp.split(h @ w_ffn_in[l], 2, axis=-1)
        x = x + rms_norm((jax.nn.silu(gate) * up) @ w_ffn_out[l], ffn_post_g[l])
    return x
```

```python
import functools

import jax
import jax.numpy as jnp
import numpy as np
from jax import lax
from jax.experimental import pallas as pl
from jax.experimental.pallas import tpu as pltpu

F32 = jnp.float32
BF16 = jnp.bfloat16
HIGHEST = lax.Precision.HIGHEST

D_MODEL = 1024
NORM_EPS = 1e-6
MASK_VALUE = -1e30
MIN_GATE = 1e-30
HG_HEADS = 8
HG_DK = 128
RW_HEADS = 16
RW_N = 64
RW_WIDTH = RW_HEADS * RW_N
RW_LORA_W = 64
RW_LORA_A = 64
RW_LORA_G = 128
RW_LORA_V = 32
RW_LNX_EPS = 1e-5 * RW_N
MLA_HEADS = 8
MLA_Q_RANK = 384
MLA_KV_RANK = 256
MLA_NOPE = 128
MLA_ROPE = 64
MLA_V = 128
ROPE_THETA = 10000.0
D_FF = 2816

LANES = 128
SUBLANES = 8
VMEM_LIMIT = 48 * 1024 * 1024

ZA_HG, ZA_RKV, ZA_GATE, ZA_WIDTH = 0, 4096, 7168, 10240
ZB_CKV, ZB_KR, ZB_CQ, ZB_WA, ZB_G, ZB_VRES = 0, 256, 384, 768, 896, 1024


def _cparams(*sem):
    return pltpu.CompilerParams(dimension_semantics=sem, vmem_limit_bytes=VMEM_LIMIT)


def _sigmoid(x):
    return 1.0 / (1.0 + jnp.exp(-x))


def _rms(x, g):
    ms = jnp.mean(x * x, axis=-1, keepdims=True)
    return x * lax.rsqrt(ms + NORM_EPS) * g


def _dot(a, b):
    return jnp.dot(a.astype(BF16), b.astype(BF16), preferred_element_type=F32)


def _dot_nt(a, b):
    return lax.dot_general(a.astype(BF16), b.astype(BF16), (((1,), (1,)), ((), ())), preferred_element_type=F32)


def _dot_tn(a, b):
    return lax.dot_general(a.astype(BF16), b.astype(BF16), (((0,), (0,)), ((), ())), preferred_element_type=F32)


def _norm_matmul_kernel(x_ref, g_ref, w_ref, o_ref, h_ref):
    @pl.when(pl.program_id(1) == 0)
    def _():
        h_ref[...] = _rms(x_ref[...], g_ref[...]).astype(BF16)

    o_ref[...] = jnp.dot(h_ref[...], w_ref[...], preferred_element_type=F32).astype(o_ref.dtype)


def norm_matmul(x, g, w, tm, tn):
    T, D = x.shape
    N = w.shape[1]
    return pl.pallas_call(
        _norm_matmul_kernel,
        grid=(T // tm, N // tn),
        in_specs=[
            pl.BlockSpec((tm, D), lambda i, j: (i, 0)),
            pl.BlockSpec((1, D), lambda i, j: (0, 0)),
            pl.BlockSpec((D, tn), lambda i, j: (0, j)),
        ],
        out_specs=pl.BlockSpec((tm, tn), lambda i, j: (i, j)),
        out_shape=jax.ShapeDtypeStruct((T, N), F32),
        scratch_shapes=[pltpu.VMEM((tm, D), BF16)],
        compiler_params=_cparams("parallel", "arbitrary"),
        name="norm_matmul",
    )(x, g, w)


HG_CHUNK = 128


def _hgrn_kernel(q_ref, f_ref, v_ref, og_ref, lb_ref, gn_ref, o_ref, st_ref):
    C = q_ref.shape[0]

    @pl.when(pl.program_id(2) == 0)
    def _():
        st_ref[...] = jnp.zeros_like(st_ref)

    q = q_ref[...]
    zf = f_ref[...]
    v = v_ref[...]
    lb = lb_ref[...]
    f = lb + (1.0 - lb) * _sigmoid(zf)
    lf = jnp.log(jnp.maximum(f, MIN_GATE))
    k = (1.0 - lb) * _sigmoid(-zf)

    row = lax.broadcasted_iota(jnp.int32, (C, C), 0)
    col = lax.broadcasted_iota(jnp.int32, (C, C), 1)
    tril = jnp.where(col <= row, 1.0, 0.0).astype(F32)
    b = jnp.dot(tril, lf, precision=HIGHEST, preferred_element_type=F32)

    ones = jnp.ones((HG_DK, C), BF16)
    scores = jnp.zeros((C, C), F32)
    g8 = C // SUBLANES
    b3 = b.reshape(g8, SUBLANES, HG_DK)
    k3 = k.reshape(g8, SUBLANES, HG_DK)
    for j in range(SUBLANES):
        bj = jnp.broadcast_to(b3[:, j:j + 1, :], (g8, SUBLANES, HG_DK)).reshape(C, HG_DK)
        kj = jnp.broadcast_to(k3[:, j:j + 1, :], (g8, SUBLANES, HG_DK)).reshape(C, HG_DK)
        xj = q * kj * jnp.exp(jnp.minimum(b - bj, 0.0))
        cj = jnp.dot(xj.astype(BF16), ones, preferred_element_type=F32)
        hit = (col == (row & -SUBLANES) + j) & ((row & (SUBLANES - 1)) >= j)
        scores = jnp.where(hit, cj, scores)
    m = SUBLANES
    rowc = lax.broadcasted_iota(jnp.int32, (C, HG_DK), 0)
    while m < C:
        pieces = []
        for p in range(C // (2 * m)):
            r0 = (2 * p + 1) * m - 1
            pieces.append(jnp.broadcast_to(b[r0:r0 + 1, :], (2 * m, HG_DK)))
        ref = pieces[0] if len(pieces) == 1 else jnp.concatenate(pieces, axis=0)
        e = jnp.exp(-jnp.abs(b - ref))
        lg = m.bit_length() - 1
        odd = ((rowc >> lg) & 1) == 1
        xm = (jnp.where(odd, q, k) * e).astype(BF16)
        gm = _dot_nt(xm, xm)
        hit = (((row >> lg) & 1) == 1) & ((col >> lg) == (row >> lg) - 1)
        scores = jnp.where(hit, gm, scores)
        m *= 2

    st = st_ref[...]
    o = _dot(scores, v) + _dot_nt(q * jnp.exp(b), st)
    bl = b[C - 1:C, :]
    st_ref[...] = st * jnp.exp(bl) + _dot_tn(v, k * jnp.exp(bl - b))

    o = _rms(o, gn_ref[...])
    og = og_ref[...]
    o_ref[...] = (o * (og * _sigmoid(og))).astype(o_ref.dtype)


def hgrn(z_a, lower_bound, onorm_g, batch, seq):
    T = z_a.shape[0]
    C = HG_CHUNK
    nc = seq // C
    cb = ZA_HG // HG_DK

    def zspec(seg):
        return pl.BlockSpec((C, HG_DK), lambda b, h, c, seg=seg: (b * nc + c, cb + seg * HG_HEADS + h))

    return pl.pallas_call(
        _hgrn_kernel,
        grid=(batch, HG_HEADS, nc),
        in_specs=[
            zspec(0), zspec(1), zspec(2), zspec(3),
            pl.BlockSpec((1, HG_DK), lambda b, h, c: (0, h)),
            pl.BlockSpec((1, HG_DK), lambda b, h, c: (0, 0)),
        ],
        out_specs=pl.BlockSpec((C, HG_DK), lambda b, h, c: (b * nc + c, h)),
        out_shape=jax.ShapeDtypeStruct((T, HG_HEADS * HG_DK), BF16),
        scratch_shapes=[pltpu.VMEM((HG_DK, HG_DK), F32)],
        compiler_params=_cparams("parallel", "parallel", "arbitrary"),
        name="hgrn",
    )(z_a, z_a, z_a, z_a, lower_bound, onorm_g)


def _head_sum(x):
    r = lax.broadcasted_iota(jnp.int32, (LANES, LANES), 0)
    c = lax.broadcasted_iota(jnp.int32, (LANES, LANES), 1)
    lg = RW_N.bit_length() - 1
    seg = jnp.where((r >> lg) == (c >> lg), 1.0, 0.0).astype(BF16)
    hi = x.astype(BF16)
    lo = (x - hi.astype(F32)).astype(BF16)
    return jnp.dot(hi, seg, preferred_element_type=F32) + jnp.dot(lo, seg, preferred_element_type=F32)


def _shift_mix(cur_ref, prev_ref, mu, first):
    z = cur_ref[...]
    last = prev_ref[SUBLANES - 1:SUBLANES, :]
    last = jnp.where(first, jnp.zeros_like(last), last)
    rolled = pltpu.roll(z, shift=1, axis=0)
    rid = lax.broadcasted_iota(jnp.int32, z.shape, 0)
    zp = jnp.where(rid == 0, jnp.broadcast_to(last, z.shape), rolled)
    return z + (zp - z) * mu


def _rw_prep_kernel(has_vres, seq, *refs):
    if has_vres:
        (r_ref, rp_ref, k_ref, kp_ref, v_ref, vp_ref, wa_ref, wap_ref, g_ref, gp_ref, zv_ref, zvp_ref, vf_ref,
         mu_r, mu_k, mu_v, mu_wa, mu_g, mu_zv, w0, w_up, a0, a_up, g_up, v0, v_up, k_k, k_a,
         ro, ldo, ko, vo, kko, ao, go) = refs
    else:
        (r_ref, rp_ref, k_ref, kp_ref, v_ref, vp_ref, wa_ref, wap_ref, g_ref, gp_ref,
         mu_r, mu_k, mu_v, mu_wa, mu_g, w0, w_up, a0, a_up, g_up, k_k, k_a,
         ro, ldo, ko, vo, kko, ao, go) = refs
    tm = r_ref.shape[0]
    first = (pl.program_id(0) * tm) % seq == 0

    r = _shift_mix(r_ref, rp_ref, mu_r[...], first)
    k = _shift_mix(k_ref, kp_ref, mu_k[...], first)
    v = _shift_mix(v_ref, vp_ref, mu_v[...], first)
    wa = _shift_mix(wa_ref, wap_ref, mu_wa[...], first)
    zg = _shift_mix(g_ref, gp_ref, mu_g[...], first)

    zw = w0[...] + _dot(jnp.tanh(wa), w_up[...])
    w_log = -(jnp.maximum(-zw, 0.0) + jnp.log(1.0 + jnp.exp(-jnp.abs(zw)))) - 0.5
    ldo[...] = -jnp.exp(w_log)
    a = _sigmoid(a0[...] + _dot(wa, a_up[...]))
    go[...] = _dot(_sigmoid(zg), g_up[...])
    if has_vres:
        zv = _shift_mix(zv_ref, zvp_ref, mu_zv[...], first)
        v = v + (vf_ref[...] - v) * _sigmoid(v0[...] + _dot(zv, v_up[...]))
    kk = k * k_k[...]
    for p in range(RW_WIDTH // LANES):
        sl = slice(p * LANES, (p + 1) * LANES)
        kkp = kk[:, sl]
        ss = _head_sum(kkp * kkp)
        kko[:, sl] = kkp * lax.rsqrt(jnp.maximum(ss, 1e-24))
    ro[...] = r
    ko[...] = k * (1.0 + (a - 1.0) * k_a[...])
    vo[...] = v
    ao[...] = a


def rw_prep(z_a, z_b, v_first, p, seq, tm):
    T = z_a.shape[0]
    has_vres = v_first is not None
    W = RW_WIDTH
    tb = tm // SUBLANES

    def cur(width, off):
        return pl.BlockSpec((tm, width), lambda i: (i, off // width))

    def prev(width, off):
        return pl.BlockSpec((SUBLANES, width), lambda i: (jnp.maximum(i * tb - 1, 0), off // width))

    def full(a):
        return pl.BlockSpec(a.shape, lambda i: (0,) * a.ndim)

    acts = [z_a, z_a, z_a, z_a, z_a, z_a, z_b, z_b, z_b, z_b]
    specs = [cur(W, ZA_RKV), prev(W, ZA_RKV), cur(W, ZA_RKV + W), prev(W, ZA_RKV + W),
             cur(W, ZA_RKV + 2 * W), prev(W, ZA_RKV + 2 * W),
             cur(LANES, ZB_WA), prev(LANES, ZB_WA), cur(LANES, ZB_G), prev(LANES, ZB_G)]
    if has_vres:
        acts += [z_b, z_b, v_first]
        specs += [cur(LANES, ZB_VRES), prev(LANES, ZB_VRES), pl.BlockSpec((tm, W), lambda i: (i, 0))]
        names = ["mu_r", "mu_k", "mu_v", "mu_wa", "mu_g", "mu_zv", "w0", "w_up", "a0", "a_up", "g_up", "v0", "v_up",
                 "k_k", "k_a"]
    else:
        names = ["mu_r", "mu_k", "mu_v", "mu_wa", "mu_g", "w0", "w_up", "a0", "a_up", "g_up", "k_k", "k_a"]
    params = [p[n] for n in names]
    out_spec = pl.BlockSpec((tm, W), lambda i: (i, 0))
    return pl.pallas_call(
        functools.partial(_rw_prep_kernel, has_vres, seq),
        grid=(T // tm,),
        in_specs=specs + [full(a) for a in params],
        out_specs=[out_spec] * 7,
        out_shape=[jax.ShapeDtypeStruct((T, W), F32)] * 7,
        compiler_params=_cparams("parallel"),
        name="rw_prep",
    )(*acts, *params)


RW_CHUNK = 64
RW_PAIRS_PER_STEP = 2


def _stack_heads(x, lane):
    return jnp.concatenate([jnp.where(lane < RW_N, x, 0.0), jnp.where(lane >= RW_N, x, 0.0)], axis=0)


def _rw_pair(r, ld, k, v, kk, a, zt):
    C = r.shape[0]
    C2 = 2 * C
    trow = lax.broadcasted_iota(jnp.int32, (C, C), 0)
    tcol = lax.broadcasted_iota(jnp.int32, (C, C), 1)
    tril = jnp.where(tcol <= trow, 1.0, 0.0).astype(F32)
    cl = jnp.dot(tril, ld, precision=HIGHEST, preferred_element_type=F32)
    gc = cl[C - 1:C, :]
    e_inv = jnp.exp(-cl)
    e_end = jnp.exp(gc - cl)
    kka = kk * a
    lane = lax.broadcasted_iota(jnp.int32, (C, LANES), 1)
    stack = functools.partial(_stack_heads, lane=lane)
    aa = stack(-kk * jnp.exp(cl - ld)).astype(BF16)
    rr = stack(r * jnp.exp(cl)).astype(BF16)
    bb = stack(kka * e_inv).astype(BF16)
    kq = stack(k * e_inv).astype(BF16)
    bg = stack(kka * e_end).astype(BF16)
    kg = stack(k * e_end).astype(BF16)
    vv = stack(v).astype(BF16)

    row = lax.broadcasted_iota(jnp.int32, (C2, C2), 0)
    col = lax.broadcasted_iota(jnp.int32, (C2, C2), 1)
    strict = (col & (C - 1)) < (row & (C - 1))
    incl = (col & (C - 1)) <= (row & (C - 1))
    lab = jnp.where(strict, _dot_nt(aa, bb), 0.0)
    lak = jnp.where(strict, _dot_nt(aa, kq), 0.0)
    mb = jnp.where(incl, _dot_nt(rr, bb), 0.0)
    mk = jnp.where(incl, _dot_nt(rr, kq), 0.0)
    p_inv = jnp.where(row == col, 1.0, 0.0) + lab
    lk = lab
    n = 1
    while 2 * n < C:
        lk = _dot(lk, lk)
        p_inv = p_inv + _dot(lk, p_inv)
        n *= 2
    ztb = zt.astype(BF16)
    uu = _dot(p_inv, _dot_nt(aa, ztb) + _dot(lak, vv))
    yy = _dot_nt(rr, ztb) + _dot(mb, uu) + _dot(mk, vv)
    zt_new = zt * jnp.exp(gc) + _dot_tn(uu, bg) + _dot_tn(vv, kg)
    return yy[:C] + yy[C:], zt_new


def _rw_chunk_kernel(r_ref, ld_ref, k_ref, v_ref, kk_ref, a_ref, g_ref, rk_ref, lg_ref, lb_ref, o_ref, st_ref):
    @pl.when(pl.program_id(2) == 0)
    def _():
        st_ref[...] = jnp.zeros_like(st_ref)

    for p in range(r_ref.shape[1] // LANES):
        sl = slice(p * LANES, (p + 1) * LANES)
        r = r_ref[:, sl]
        k = k_ref[:, sl]
        v = v_ref[:, sl]
        y, zt = _rw_pair(r, ld_ref[:, sl], k, v, kk_ref[:, sl], a_ref[:, sl], st_ref[p])
        st_ref[p] = zt
        mean = _head_sum(y) * (1.0 / RW_N)
        d = y - mean
        var = _head_sum(d * d) * (1.0 / RW_N)
        yn = d * lax.rsqrt(var + RW_LNX_EPS) * lg_ref[:, sl] + lb_ref[:, sl]
        bonus = _head_sum(r * k * rk_ref[:, sl]) * v
        o_ref[:, sl] = ((yn + bonus) * g_ref[:, sl]).astype(o_ref.dtype)


def rw_chunk(r, ld, k, v, kk, a, g, r_k, lnx_g, lnx_b, batch, seq):
    T, W = r.shape
    C = RW_CHUNK
    nc = seq // C
    wb = RW_PAIRS_PER_STEP * LANES
    act = pl.BlockSpec((C, wb), lambda b, h, c: (b * nc + c, h))
    par = pl.BlockSpec((1, wb), lambda b, h, c: (0, h))
    return pl.pallas_call(
        _rw_chunk_kernel,
        grid=(batch, W // wb, nc),
        in_specs=[act] * 7 + [par] * 3,
        out_specs=act,
        out_shape=jax.ShapeDtypeStruct((T, W), BF16),
        scratch_shapes=[pltpu.VMEM((RW_PAIRS_PER_STEP, LANES, LANES), F32)],
        compiler_params=_cparams("parallel", "parallel", "arbitrary"),
        name="rw_chunk",
    )(r, ld, k, v, kk, a, g, r_k, lnx_g, lnx_b)


MLA_QK = 2 * LANES


def _mla_up_kernel(cq_ref, ckv_ref, kr_ref, cs_ref, gq_ref, gkv_ref, wq_ref, wkv_ref, q_ref, kv_ref, krd_ref):
    scale = (MLA_NOPE + MLA_ROPE) ** -0.5
    cs = cs_ref[...]
    q = _dot(_rms(cq_ref[...], gq_ref[...]), wq_ref[...])
    for h in range(MLA_HEADS):
        o = h * MLA_QK
        q_ref[:, o:o + LANES] = (q[:, o:o + LANES] * scale).astype(q_ref.dtype)
        q_ref[:, o + LANES:o + 2 * LANES] = (q[:, o + LANES:o + 2 * LANES] * (cs * scale)).astype(q_ref.dtype)
    kv_ref[...] = _dot(_rms(ckv_ref[...], gkv_ref[...]), wkv_ref[...]).astype(kv_ref.dtype)
    krx = kr_ref[...] * cs
    krd_ref[...] = (krx + pltpu.roll(krx, shift=MLA_ROPE, axis=1)).astype(krd_ref.dtype)


def mla_up(z_b, cs, gq, gkv, wq, wkv, tm):
    T = z_b.shape[0]
    nq = wq.shape[1]
    nkv = wkv.shape[1]

    def full(a):
        return pl.BlockSpec(a.shape, lambda i: (0,) * a.ndim)

    return pl.pallas_call(
        _mla_up_kernel,
        grid=(T // tm,),
        in_specs=[
            pl.BlockSpec((tm, MLA_Q_RANK), lambda i: (i, ZB_CQ // MLA_Q_RANK)),
            pl.BlockSpec((tm, MLA_KV_RANK), lambda i: (i, ZB_CKV // MLA_KV_RANK)),
            pl.BlockSpec((tm, LANES), lambda i: (i, ZB_KR // LANES)),
            pl.BlockSpec((tm, LANES), lambda i: (i, 0)),
            full(gq), full(gkv), full(wq), full(wkv),
        ],
        out_specs=[
            pl.BlockSpec((tm, nq), lambda i: (i, 0)),
            pl.BlockSpec((tm, nkv), lambda i: (i, 0)),
            pl.BlockSpec((tm, LANES), lambda i: (i, 0)),
        ],
        out_shape=[
            jax.ShapeDtypeStruct((T, nq), BF16),
            jax.ShapeDtypeStruct((T, nkv), BF16),
            jax.ShapeDtypeStruct((T, LANES), BF16),
        ],
        compiler_params=_cparams("parallel"),
        name="mla_up",
    )(z_b, z_b, z_b, cs, gq, gkv, wq, wkv)


MLA_TQ = 512


def _mla_attn_kernel(q_ref, kn_ref, kr_ref, v_ref, o_ref, m_ref, l_ref, acc_ref):
    tq = q_ref.shape[0]
    qi = pl.program_id(2)
    q = q_ref[...]
    m_ref[...] = jnp.full_like(m_ref, -jnp.inf)
    l_ref[...] = jnp.zeros_like(l_ref)
    acc_ref[...] = jnp.zeros_like(acc_ref)

    def block(kb, masked):
        ks = pl.multiple_of(kb * tq, tq)
        kc = jnp.concatenate([kn_ref[pl.ds(ks, tq), :], kr_ref[pl.ds(ks, tq), :]], axis=1)
        s = lax.dot_general(q, kc, (((1,), (1,)), ((), ())), preferred_element_type=F32)
        if masked:
            row = lax.broadcasted_iota(jnp.int32, s.shape, 0)
            col = lax.broadcasted_iota(jnp.int32, s.shape, 1)
            s = jnp.where(col <= row, s, MASK_VALUE)
        m_old = m_ref[...]
        m_new = jnp.maximum(m_old, jnp.max(s, axis=-1, keepdims=True))
        alpha = jnp.exp(m_old - m_new)
        p = jnp.exp(s - m_new)
        l_ref[...] = alpha * l_ref[...] + jnp.sum(p, axis=-1, keepdims=True)
        acc_ref[...] = alpha * acc_ref[...] + jnp.dot(p.astype(BF16), v_ref[pl.ds(ks, tq), :],
                                                       preferred_element_type=F32)
        m_ref[...] = m_new

    def body(kb, carry):
        block(kb, False)
        return carry

    lax.fori_loop(0, qi, body, 0)
    block(qi, True)
    o_ref[...] = (acc_ref[...] / l_ref[...]).astype(o_ref.dtype)


def mla_attn(q, kv, krd, batch, seq):
    T = q.shape[0]
    tq = MLA_TQ
    nq = seq // tq
    return pl.pallas_call(
        _mla_attn_kernel,
        grid=(batch, MLA_HEADS, nq),
        in_specs=[
            pl.BlockSpec((tq, MLA_QK), lambda b, h, i: (b * nq + i, h)),
            pl.BlockSpec((seq, MLA_NOPE), lambda b, h, i: (b, 2 * h)),
            pl.BlockSpec((seq, LANES), lambda b, h, i: (b, 0)),
            pl.BlockSpec((seq, MLA_V), lambda b, h, i: (b, 2 * h + 1)),
        ],
        out_specs=pl.BlockSpec((tq, MLA_V), lambda b, h, i: (b * nq + i, h)),
        out_shape=jax.ShapeDtypeStruct((T, MLA_HEADS * MLA_V), BF16),
        scratch_shapes=[pltpu.VMEM((tq, 1), F32), pltpu.VMEM((tq, 1), F32), pltpu.VMEM((tq, MLA_V), F32)],
        compiler_params=_cparams("parallel", "parallel", "arbitrary"),
        name="mla_attn",
    )(q, kv, krd, kv)


def _merge_kernel(x_ref, ohg_ref, orw_ref, omla_ref, g0_ref, g1_ref, g2_ref, wb_ref, wo_ref, gp_ref, o_ref):
    merged = (_sigmoid(g0_ref[...]) * jnp.dot(ohg_ref[...], wb_ref[0], preferred_element_type=F32)
              + _sigmoid(g1_ref[...]) * jnp.dot(orw_ref[...], wb_ref[1], preferred_element_type=F32)
              + _sigmoid(g2_ref[...]) * jnp.dot(omla_ref[...], wb_ref[2], preferred_element_type=F32))
    y = _dot(merged, wo_ref[...])
    o_ref[...] = x_ref[...] + _rms(y, gp_ref[...])


def merge(x, o_hg, o_rw, o_mla, z_a, w_branch, w_out, g_post, tm):
    T, D = x.shape
    row = pl.BlockSpec((tm, D), lambda i: (i, 0))

    def gate(n):
        return pl.BlockSpec((tm, D), lambda i, n=n: (i, ZA_GATE // D + n))

    def full(a):
        return pl.BlockSpec(a.shape, lambda i: (0,) * a.ndim)

    return pl.pallas_call(
        _merge_kernel,
        grid=(T // tm,),
        in_specs=[row, row, row, row, gate(0), gate(1), gate(2), full(w_branch), full(w_out), full(g_post)],
        out_specs=row,
        out_shape=jax.ShapeDtypeStruct((T, D), F32),
        compiler_params=_cparams("parallel"),
        name="merge",
    )(x, o_hg, o_rw, o_mla, z_a, z_a, z_a, w_branch, w_out, g_post)


FFN_TILE = 256


def _ffn_kernel(x_ref, gpre_ref, wg_ref, wu_ref, wo_ref, gpost_ref, o_ref, h_ref, acc_ref):
    j = pl.program_id(1)

    @pl.when(j == 0)
    def _():
        h_ref[...] = _rms(x_ref[...], gpre_ref[...]).astype(BF16)
        acc_ref[...] = jnp.zeros_like(acc_ref)

    h = h_ref[...]
    gate = jnp.dot(h, wg_ref[...], preferred_element_type=F32)
    up = jnp.dot(h, wu_ref[...], preferred_element_type=F32)
    act = gate * _sigmoid(gate) * up
    acc_ref[...] += _dot(act, wo_ref[...])

    @pl.when(j == pl.num_programs(1) - 1)
    def _():
        o_ref[...] = x_ref[...] + _rms(acc_ref[...], gpost_ref[...])


def ffn(x, g_pre, w_in, w_out, g_post, tm):
    T, D = x.shape
    tf = FFN_TILE
    nf = D_FF // tf
    return pl.pallas_call(
        _ffn_kernel,
        grid=(T // tm, nf),
        in_specs=[
            pl.BlockSpec((tm, D), lambda i, j: (i, 0)),
            pl.BlockSpec((1, D), lambda i, j: (0, 0)),
            pl.BlockSpec((D, tf), lambda i, j: (0, j)),
            pl.BlockSpec((D, tf), lambda i, j: (0, nf + j)),
            pl.BlockSpec((tf, D), lambda i, j: (j, 0)),
            pl.BlockSpec((1, D), lambda i, j: (0, 0)),
        ],
        out_specs=pl.BlockSpec((tm, D), lambda i, j: (i, 0)),
        out_shape=jax.ShapeDtypeStruct((T, D), F32),
        scratch_shapes=[pltpu.VMEM((tm, D), BF16), pltpu.VMEM((tm, D), F32)],
        compiler_params=_cparams("parallel", "arbitrary"),
        name="ffn",
    )(x, g_pre, w_in, w_in, w_out, g_post)


def _swap_halves(w):
    h = w.shape[-1] // 2
    return jnp.concatenate([w[..., h:], w[..., :h]], axis=-1)


def _pad_rows(w, rows, at):
    out = jnp.zeros((rows, w.shape[1]), w.dtype)
    return out.at[at:at + w.shape[0]].set(w)


def _pack_in_proj(w, w_vres):
    hg, rkv, wa, g, cq, ckv, kr, gates = jnp.split(
        w, np.cumsum([4096, 3072, RW_LORA_W + RW_LORA_A, RW_LORA_G, MLA_Q_RANK, MLA_KV_RANK, MLA_ROPE]).tolist(),
        axis=1)
    wide = jnp.concatenate([hg, rkv, gates], axis=1)
    parts = [ckv, kr, _swap_halves(kr), cq, wa, g]
    if w_vres is not None:
        parts.append(jnp.pad(w_vres, ((0, 0), (0, LANES - RW_LORA_V))))
    narrow = jnp.concatenate(parts, axis=1)
    return wide.astype(BF16), narrow.astype(BF16)


def _pack_wq(w_uq):
    w = w_uq.reshape(MLA_Q_RANK, MLA_HEADS, MLA_NOPE + MLA_ROPE)
    rope = w[..., MLA_NOPE:]
    w = jnp.concatenate([w[..., :MLA_NOPE], rope, _swap_halves(rope)], axis=-1)
    return w.reshape(MLA_Q_RANK, MLA_HEADS * MLA_QK).astype(BF16)


def _row(v):
    return v.reshape(1, -1).astype(F32)


def kernel(x, positions, hgrn_lb_logits, mix_pre_g, mix_post_g, ffn_pre_g, ffn_post_g, w_in, w_vres_down,
           hgrn_onorm_g, rwkv_mu, rwkv_vres_mu, rwkv_w0, rwkv_w_up, rwkv_a0, rwkv_a_up, rwkv_g_up, rwkv_v0,
           rwkv_v_up, rwkv_k_k, rwkv_k_a, rwkv_r_k, rwkv_lnx_g, rwkv_lnx_b, mla_q_norm_g, mla_w_uq,
           mla_kv_norm_g, mla_w_ukv, w_branch, w_out, w_ffn_in, w_ffn_out):
    batch, seq, d = x.shape
    depth = w_in.shape[0]
    T = batch * seq
    xt = x.reshape(T, d)

    probs = jax.nn.softmax(hgrn_lb_logits.astype(F32), axis=0)
    lower_bounds = jnp.cumsum(probs, axis=0) - probs[0]
    inv_freq = ROPE_THETA ** (-jnp.arange(0, MLA_ROPE, 2, dtype=F32) / MLA_ROPE)
    ang = positions.astype(F32).reshape(T, 1) * inv_freq
    cos, sin = jnp.cos(ang), jnp.sin(ang)
    cs = jnp.concatenate([cos, cos, -sin, sin], axis=-1)

    W = RW_WIDTH
    v_first = None
    for l in range(depth):
        w_wide, w_narrow = _pack_in_proj(w_in[l], None if l == 0 else w_vres_down[l - 1])
        g_pre = _row(mix_pre_g[l])
        z_a = norm_matmul(xt, g_pre, w_wide, 1024, 1024)
        z_b = norm_matmul(xt, g_pre, w_narrow, 1024, w_narrow.shape[1])

        o_hg = hgrn(z_a, _row(lower_bounds[l]), _row(hgrn_onorm_g[l]), batch, seq)

        mu = rwkv_mu[l]
        p = {
            "mu_r": _row(mu[:W]), "mu_k": _row(mu[W:2 * W]), "mu_v": _row(mu[2 * W:3 * W]),
            "mu_wa": _row(mu[3 * W:3 * W + LANES]), "mu_g": _row(mu[3 * W + LANES:]),
            "w0": _row(rwkv_w0[l]), "w_up": _pad_rows(rwkv_w_up[l], LANES, 0).astype(BF16),
            "a0": _row(rwkv_a0[l]), "a_up": _pad_rows(rwkv_a_up[l], LANES, RW_LORA_W).astype(BF16),
            "g_up": rwkv_g_up[l].astype(BF16),
            "k_k": _row(rwkv_k_k[l]), "k_a": _row(rwkv_k_a[l]),
        }
        if l > 0:
            p["mu_zv"] = _row(jnp.pad(rwkv_vres_mu[l - 1], (0, LANES - RW_LORA_V)))
            p["v0"] = _row(rwkv_v0[l - 1])
            p["v_up"] = _pad_rows(rwkv_v_up[l - 1], LANES, 0).astype(BF16)
        r, ld, k, v, kk, a, g = rw_prep(z_a, z_b, v_first, p, seq, 256)
        if l == 0:
            v_first = v
        o_rw = rw_chunk(r, ld, k, v, kk, a, g, _row(rwkv_r_k[l]), _row(rwkv_lnx_g[l]), _row(rwkv_lnx_b[l]),
                        batch, seq)

        q, kv, krd = mla_up(z_b, cs, _row(mla_q_norm_g[l]), _row(mla_kv_norm_g[l]), _pack_wq(mla_w_uq[l]),
                            mla_w_ukv[l].astype(BF16), 512)
        o_mla = mla_attn(q, kv, krd, batch, seq)

        xt = merge(xt, o_hg, o_rw, o_mla, z_a, w_branch[l].astype(BF16), w_out[l].astype(BF16),
                   _row(mix_post_g[l]), 256)
        xt = ffn(xt, _row(ffn_pre_g[l]), w_ffn_in[l].astype(BF16), w_ffn_out[l].astype(BF16),
                 _row(ffn_post_g[l]), 1024)
    return xt.reshape(batch, seq, d)
```

```python
import functools
import math

import jax
import jax.numpy as jnp
import numpy as np
from jax import lax
from jax.experimental import pallas as pl
from jax.experimental.pallas import tpu as pltpu

F32 = jnp.float32
BF16 = jnp.bfloat16
LOG2E = math.log2(math.e)

D_MODEL = 1024
NORM_EPS = 1e-6
MASK_VALUE = -1e30
MIN_GATE = 1e-30
HG_HEADS = 8
HG_DK = 128
RW_HEADS = 16
RW_N = 64
RW_WIDTH = RW_HEADS * RW_N
RW_LORA_W = 64
RW_LORA_A = 64
RW_LORA_G = 128
RW_LORA_V = 32
RW_LNX_EPS = 1e-5 * RW_N
MLA_HEADS = 8
MLA_Q_RANK = 384
MLA_KV_RANK = 256
MLA_NOPE = 128
MLA_ROPE = 64
MLA_V = 128
ROPE_THETA = 10000.0
D_FF = 2816

LANES = 128
SUBLANES = 8
VMEM_LIMIT = 48 * 1024 * 1024

PROJ_ROWS = 1024
PROJ_COLS = 1024
RW_PREP_ROWS = 256
MLA_UP_ROWS = 512
MERGE_ROWS = 256
FFN_ROWS = 1024

ZA_HG, ZA_RKV, ZA_GATE, ZA_WIDTH = 0, 4096, 7168, 10240
ZB_CKV, ZB_KR, ZB_CQ, ZB_WA, ZB_G, ZB_VRES = 0, 256, 384, 768, 896, 1024


def _cparams(*sem):
    return pltpu.CompilerParams(dimension_semantics=sem, vmem_limit_bytes=VMEM_LIMIT)


def _sigmoid(x):
    return 1.0 / (1.0 + jnp.exp(-x))


def _rms(x, g):
    ms = jnp.mean(x * x, axis=-1, keepdims=True)
    return x * lax.rsqrt(ms + NORM_EPS) * g


def _dot(a, b):
    return jnp.dot(a.astype(BF16), b.astype(BF16), preferred_element_type=F32)


def _dot_nt(a, b):
    return lax.dot_general(a.astype(BF16), b.astype(BF16), (((1,), (1,)), ((), ())), preferred_element_type=F32)


def _dot_tn(a, b):
    return lax.dot_general(a.astype(BF16), b.astype(BF16), (((0,), (0,)), ((), ())), preferred_element_type=F32)


def _tril_ones(n):
    row = lax.broadcasted_iota(jnp.int32, (n, n), 0)
    col = lax.broadcasted_iota(jnp.int32, (n, n), 1)
    return jnp.where(col <= row, 1.0, 0.0).astype(BF16)


def _cumsum_rows(tril, x):
    n = x.shape[1]
    hi = x.astype(BF16)
    r1 = x - hi.astype(F32)
    mid = r1.astype(BF16)
    lo = (r1 - mid.astype(F32)).astype(BF16)
    y = jnp.dot(tril, jnp.concatenate([hi, mid, lo], axis=1), preferred_element_type=F32)
    return y[:, :n] + y[:, n:2 * n] + y[:, 2 * n:]


def _norm_matmul_kernel(x_ref, g_ref, w_ref, o_ref, h_ref):
    @pl.when(pl.program_id(1) == 0)
    def _():
        h_ref[...] = _rms(x_ref[...], g_ref[...]).astype(BF16)

    o_ref[...] = jnp.dot(h_ref[...], w_ref[...], preferred_element_type=F32).astype(o_ref.dtype)


def norm_matmul(x, g, w, tn):
    T, D = x.shape
    N = w.shape[1]
    tm = PROJ_ROWS
    return pl.pallas_call(
        _norm_matmul_kernel,
        grid=(T // tm, N // tn),
        in_specs=[
            pl.BlockSpec((tm, D), lambda i, j: (i, 0)),
            pl.BlockSpec((1, D), lambda i, j: (0, 0)),
            pl.BlockSpec((D, tn), lambda i, j: (0, j)),
        ],
        out_specs=pl.BlockSpec((tm, tn), lambda i, j: (i, j)),
        out_shape=jax.ShapeDtypeStruct((T, N), F32),
        scratch_shapes=[pltpu.VMEM((tm, D), BF16)],
        compiler_params=_cparams("parallel", "arbitrary"),
        name="norm_matmul",
    )(x, g, w)


HG_CHUNK = 128


def _hgrn_term_codes(C):
    t = np.arange(C)[:, None]
    s = np.arange(C)[None, :]
    codes = np.full((C, C), -1, np.int32)
    same8 = (t // SUBLANES == s // SUBLANES) & (s <= t)
    codes = np.where(same8, s % SUBLANES, codes)
    m, i = SUBLANES, 0
    while m < C:
        hit = ((t // m) % 2 == 1) & (s // m == t // m - 1)
        codes = np.where(hit, SUBLANES + i, codes)
        m, i = 2 * m, i + 1
    return codes.astype(np.int32)


def _hgrn_kernel(q_ref, f_ref, v_ref, og_ref, lb_ref, gn_ref, code_ref, o_ref, st_ref):
    C = q_ref.shape[0]
    heads = range(q_ref.shape[1] // HG_DK)

    @pl.when(pl.program_id(1) == 0)
    def _():
        st_ref[...] = jnp.zeros_like(st_ref)

    def hs(ref, h):
        return ref[:, h * HG_DK:(h + 1) * HG_DK]

    code = code_ref[...]
    tril = _tril_ones(C)
    ones = jnp.ones((HG_DK, C), BF16)
    g8 = C // SUBLANES

    q = [hs(q_ref, h) for h in heads]
    zf = [hs(f_ref, h) for h in heads]
    v = [hs(v_ref, h) for h in heads]
    lb = [hs(lb_ref, h) for h in heads]
    lf = [jnp.log2(jnp.maximum(lb[h] + (1.0 - lb[h]) * _sigmoid(zf[h]), MIN_GATE)) for h in heads]
    k = [(1.0 - lb[h]) * _sigmoid(-zf[h]) for h in heads]
    b = [_cumsum_rows(tril, lf[h]) for h in heads]

    scores = [jnp.zeros((C, C), F32) for _ in heads]
    b3 = [b[h].reshape(g8, SUBLANES, HG_DK) for h in heads]
    k3 = [k[h].reshape(g8, SUBLANES, HG_DK) for h in heads]
    for j in range(SUBLANES):
        bj = [jnp.broadcast_to(b3[h][:, j:j + 1, :], (g8, SUBLANES, HG_DK)).reshape(C, HG_DK) for h in heads]
        kj = [jnp.broadcast_to(k3[h][:, j:j + 1, :], (g8, SUBLANES, HG_DK)).reshape(C, HG_DK) for h in heads]
        xj = [(q[h] * kj[h] * jnp.exp2(jnp.minimum(b[h] - bj[h], 0.0))).astype(BF16) for h in heads]
        cj = [jnp.dot(xj[h], ones, preferred_element_type=F32) for h in heads]
        scores = [jnp.where(code == j, cj[h], scores[h]) for h in heads]
    rowc = lax.broadcasted_iota(jnp.int32, (C, HG_DK), 0)
    m, i = SUBLANES, 0
    while m < C:
        odd = ((rowc >> (m.bit_length() - 1)) & 1) == 1
        xm = []
        for h in heads:
            pieces = [jnp.broadcast_to(b[h][r0:r0 + 1, :], (2 * m, HG_DK)) for r0 in range(m - 1, C, 2 * m)]
            ref = pieces[0] if len(pieces) == 1 else jnp.concatenate(pieces, axis=0)
            d = b[h] - ref
            xm.append((jnp.where(odd, q[h], k[h]) * jnp.exp2(jnp.minimum(d, -d))).astype(BF16))
        gm = [_dot_nt(xm[h], xm[h]) for h in heads]
        scores = [jnp.where(code == SUBLANES + i, gm[h], scores[h]) for h in heads]
        m, i = 2 * m, i + 1

    st = [st_ref[h] for h in heads]
    o = [_dot(scores[h], v[h]) + _dot_nt(q[h] * jnp.exp2(b[h]), st[h]) for h in heads]
    bl = [b[h][C - 1:C, :] for h in heads]
    for h in heads:
        st_ref[h] = st[h] * jnp.exp2(bl[h]) + _dot_tn(v[h], k[h] * jnp.exp2(bl[h] - b[h]))
    for h in heads:
        og = hs(og_ref, h)
        o_ref[:, h * HG_DK:(h + 1) * HG_DK] = (_rms(o[h], gn_ref[...]) * (og * _sigmoid(og))).astype(o_ref.dtype)


def hgrn(z_a, lower_bound, onorm_g, batch, seq):
    T = z_a.shape[0]
    C = HG_CHUNK
    nc = seq // C
    W = HG_HEADS * HG_DK
    codes = jnp.asarray(_hgrn_term_codes(C))

    def zspec(seg):
        return pl.BlockSpec((C, W), lambda b, c, seg=seg: (b * nc + c, ZA_HG // W + seg))

    return pl.pallas_call(
        _hgrn_kernel,
        grid=(batch, nc),
        in_specs=[
            zspec(0), zspec(1), zspec(2), zspec(3),
            pl.BlockSpec((1, W), lambda b, c: (0, 0)),
            pl.BlockSpec((1, HG_DK), lambda b, c: (0, 0)),
            pl.BlockSpec((C, C), lambda b, c: (0, 0)),
        ],
        out_specs=pl.BlockSpec((C, W), lambda b, c: (b * nc + c, 0)),
        out_shape=jax.ShapeDtypeStruct((T, W), BF16),
        scratch_shapes=[pltpu.VMEM((HG_HEADS, HG_DK, HG_DK), F32)],
        compiler_params=_cparams("parallel", "arbitrary"),
        name="hgrn",
    )(z_a, z_a, z_a, z_a, lower_bound, onorm_g, codes)


def _head_seg():
    r = lax.broadcasted_iota(jnp.int32, (LANES, LANES), 0)
    c = lax.broadcasted_iota(jnp.int32, (LANES, LANES), 1)
    lg = RW_N.bit_length() - 1
    return jnp.where((r >> lg) == (c >> lg), 1.0, 0.0).astype(BF16)


def _head_sum(x, seg):
    hi = x.astype(BF16)
    lo = (x - hi.astype(F32)).astype(BF16)
    return jnp.dot(hi, seg, preferred_element_type=F32) + jnp.dot(lo, seg, preferred_element_type=F32)


def _shift_mix(cur_ref, prev_ref, mu, first):
    z = cur_ref[...]
    last = prev_ref[SUBLANES - 1:SUBLANES, :]
    last = jnp.where(first, jnp.zeros_like(last), last)
    rolled = pltpu.roll(z, shift=1, axis=0)
    rid = lax.broadcasted_iota(jnp.int32, z.shape, 0)
    zp = jnp.where(rid == 0, jnp.broadcast_to(last, z.shape), rolled)
    return z + (zp - z) * mu


def _rw_prep_kernel(has_vres, seq, *refs):
    if has_vres:
        (r_ref, rp_ref, k_ref, kp_ref, v_ref, vp_ref, wa_ref, wap_ref, g_ref, gp_ref, zv_ref, zvp_ref, vf_ref,
         mu_r, mu_k, mu_v, mu_wa, mu_g, mu_zv, w0, w_up, a0, a_up, g_up, v0, v_up, k_k, k_a,
         ro, ldo, ko, vo, kko, ao, go) = refs
    else:
        (r_ref, rp_ref, k_ref, kp_ref, v_ref, vp_ref, wa_ref, wap_ref, g_ref, gp_ref,
         mu_r, mu_k, mu_v, mu_wa, mu_g, w0, w_up, a0, a_up, g_up, k_k, k_a,
         ro, ldo, ko, vo, kko, ao, go) = refs
    tm = r_ref.shape[0]
    first = (pl.program_id(0) * tm) % seq == 0

    r = _shift_mix(r_ref, rp_ref, mu_r[...], first)
    k = _shift_mix(k_ref, kp_ref, mu_k[...], first)
    v = _shift_mix(v_ref, vp_ref, mu_v[...], first)
    wa = _shift_mix(wa_ref, wap_ref, mu_wa[...], first)
    zg = _shift_mix(g_ref, gp_ref, mu_g[...], first)

    zw = w0[...] + _dot(jnp.tanh(wa), w_up[...])
    w_log = -(jnp.maximum(-zw, 0.0) + jnp.log(1.0 + jnp.exp(-jnp.abs(zw)))) - 0.5
    ldo[...] = -jnp.exp(w_log)
    a = _sigmoid(a0[...] + _dot(wa, a_up[...]))
    go[...] = _dot(_sigmoid(zg), g_up[...])
    if has_vres:
        zv = _shift_mix(zv_ref, zvp_ref, mu_zv[...], first)
        v = v + (vf_ref[...] - v) * _sigmoid(v0[...] + _dot(zv, v_up[...]))
    kk = k * k_k[...]
    seg = _head_seg()
    for p in range(RW_WIDTH // LANES):
        sl = slice(p * LANES, (p + 1) * LANES)
        kkp = kk[:, sl]
        ss = _head_sum(kkp * kkp, seg)
        kko[:, sl] = kkp * lax.rsqrt(jnp.maximum(ss, 1e-24))
    ro[...] = r
    ko[...] = k * (1.0 + (a - 1.0) * k_a[...])
    vo[...] = v
    ao[...] = a


def rw_prep(z_a, z_b, v_first, p, seq):
    T = z_a.shape[0]
    has_vres = v_first is not None
    W = RW_WIDTH
    tm = RW_PREP_ROWS
    tb = tm // SUBLANES

    def cur(width, off):
        return pl.BlockSpec((tm, width), lambda i: (i, off // width))

    def prev(width, off):
        return pl.BlockSpec((SUBLANES, width), lambda i: (jnp.maximum(i * tb - 1, 0), off // width))

    def full(a):
        return pl.BlockSpec(a.shape, lambda i: (0,) * a.ndim)

    acts = [z_a, z_a, z_a, z_a, z_a, z_a, z_b, z_b, z_b, z_b]
    specs = [cur(W, ZA_RKV), prev(W, ZA_RKV), cur(W, ZA_RKV + W), prev(W, ZA_RKV + W),
             cur(W, ZA_RKV + 2 * W), prev(W, ZA_RKV + 2 * W),
             cur(LANES, ZB_WA), prev(LANES, ZB_WA), cur(LANES, ZB_G), prev(LANES, ZB_G)]
    if has_vres:
        acts += [z_b, z_b, v_first]
        specs += [cur(LANES, ZB_VRES), prev(LANES, ZB_VRES), pl.BlockSpec((tm, W), lambda i: (i, 0))]
        names = ["mu_r", "mu_k", "mu_v", "mu_wa", "mu_g", "mu_zv", "w0", "w_up", "a0", "a_up", "g_up", "v0", "v_up",
                 "k_k", "k_a"]
    else:
        names = ["mu_r", "mu_k", "mu_v", "mu_wa", "mu_g", "w0", "w_up", "a0", "a_up", "g_up", "k_k", "k_a"]
    params = [p[n] for n in names]
    out_spec = pl.BlockSpec((tm, W), lambda i: (i, 0))
    return pl.pallas_call(
        functools.partial(_rw_prep_kernel, has_vres, seq),
        grid=(T // tm,),
        in_specs=specs + [full(a) for a in params],
        out_specs=[out_spec] * 7,
        out_shape=[jax.ShapeDtypeStruct((T, W), F32)] * 7,
        compiler_params=_cparams("parallel"),
        name="rw_prep",
    )(*acts, *params)


RW_CHUNK = 64


def _rw_chunk_kernel(r_ref, ld_ref, k_ref, v_ref, kk_ref, a_ref, g_ref, rk_ref, lg_ref, lb_ref, o_ref, st_ref):
    C = r_ref.shape[0]
    C2 = 2 * C
    pairs = range(r_ref.shape[1] // LANES)

    @pl.when(pl.program_id(1) == 0)
    def _():
        st_ref[...] = jnp.zeros_like(st_ref)

    def ps(ref, p):
        return ref[:, p * LANES:(p + 1) * LANES]

    tril = _tril_ones(C)
    seg = _head_seg()
    lane = lax.broadcasted_iota(jnp.int32, (C, LANES), 1)
    row = lax.broadcasted_iota(jnp.int32, (C2, C2), 0)
    col = lax.broadcasted_iota(jnp.int32, (C2, C2), 1)
    strict = (col & (C - 1)) < (row & (C - 1))
    incl = (col & (C - 1)) <= (row & (C - 1))
    eye = jnp.where(row == col, 1.0, 0.0)

    def stack(x):
        return jnp.concatenate([jnp.where(lane < RW_N, x, 0.0), jnp.where(lane >= RW_N, x, 0.0)], axis=0).astype(BF16)

    r = [ps(r_ref, p) for p in pairs]
    ld = [ps(ld_ref, p) for p in pairs]
    k = [ps(k_ref, p) for p in pairs]
    v = [ps(v_ref, p) for p in pairs]
    kk = [ps(kk_ref, p) for p in pairs]
    kka = [kk[p] * ps(a_ref, p) for p in pairs]
    cl = [_cumsum_rows(tril, ld[p]) for p in pairs]
    gc = [cl[p][C - 1:C, :] for p in pairs]
    e_inv = [jnp.exp(-cl[p]) for p in pairs]
    e_end = [jnp.exp(gc[p] - cl[p]) for p in pairs]
    ar = [jnp.concatenate([stack(-kk[p] * jnp.exp(cl[p] - ld[p])), stack(r[p] * jnp.exp(cl[p]))], axis=0)
          for p in pairs]
    bk = [jnp.concatenate([stack(kka[p] * e_inv[p]), stack(k[p] * e_inv[p])], axis=0) for p in pairs]
    bkg = [jnp.concatenate([stack(kka[p] * e_end[p]), stack(k[p] * e_end[p])], axis=0) for p in pairs]
    vv = [stack(v[p]) for p in pairs]

    g4 = [_dot_nt(ar[p], bk[p]) for p in pairs]
    lab = [jnp.where(strict, g4[p][:C2, :C2], 0.0) for p in pairs]
    lkm = [jnp.concatenate([jnp.where(strict, g4[p][:C2, C2:], 0.0), jnp.where(incl, g4[p][C2:, C2:], 0.0)],
                           axis=0).astype(BF16) for p in pairs]
    mb = [jnp.where(incl, g4[p][C2:, :C2], 0.0).astype(BF16) for p in pairs]
    qi = [eye + lab[p] for p in pairs]
    mp = [_dot(lab[p], lab[p]) for p in pairs]
    n = 2
    while 2 * n < C:
        mq = [_dot(mp[p], jnp.concatenate([mp[p].astype(BF16), qi[p].astype(BF16)], axis=1)) for p in pairs]
        mp = [mq[p][:, :C2] for p in pairs]
        qi = [qi[p] + mq[p][:, C2:] for p in pairs]
        n *= 2
    qi = [qi[p] + _dot(mp[p], qi[p]) for p in pairs]

    zt = [st_ref[p] for p in pairs]
    ars = [_dot_nt(ar[p], zt[p]) for p in pairs]
    lkv = [_dot(lkm[p], vv[p]) for p in pairs]
    uu = [_dot(qi[p], ars[p][:C2] + lkv[p][:C2]) for p in pairs]
    yy = [ars[p][C2:] + lkv[p][C2:] + _dot(mb[p], uu[p]) for p in pairs]
    for p in pairs:
        uv = jnp.concatenate([uu[p].astype(BF16), vv[p]], axis=0)
        st_ref[p] = zt[p] * jnp.exp(gc[p]) + _dot_tn(uv, bkg[p])

    y = [yy[p][:C] + yy[p][C:] for p in pairs]
    mean = [_head_sum(y[p], seg) * (1.0 / RW_N) for p in pairs]
    d = [y[p] - mean[p] for p in pairs]
    var = [_head_sum(d[p] * d[p], seg) * (1.0 / RW_N) for p in pairs]
    bonus = [_head_sum(r[p] * k[p] * ps(rk_ref, p), seg) * v[p] for p in pairs]
    for p in pairs:
        yn = d[p] * lax.rsqrt(var[p] + RW_LNX_EPS) * ps(lg_ref, p) + ps(lb_ref, p)
        o_ref[:, p * LANES:(p + 1) * LANES] = ((yn + bonus[p]) * ps(g_ref, p)).astype(o_ref.dtype)


def rw_chunk(r, ld, k, v, kk, a, g, r_k, lnx_g, lnx_b, batch, seq):
    T, W = r.shape
    C = RW_CHUNK
    nc = seq // C
    act = pl.BlockSpec((C, W), lambda b, c: (b * nc + c, 0))
    par = pl.BlockSpec((1, W), lambda b, c: (0, 0))
    return pl.pallas_call(
        _rw_chunk_kernel,
        grid=(batch, nc),
        in_specs=[act] * 7 + [par] * 3,
        out_specs=act,
        out_shape=jax.ShapeDtypeStruct((T, W), BF16),
        scratch_shapes=[pltpu.VMEM((W // LANES, LANES, LANES), F32)],
        compiler_params=_cparams("parallel", "arbitrary"),
        name="rw_chunk",
    )(r, ld, k, v, kk, a, g, r_k, lnx_g, lnx_b)


MLA_QK = 2 * LANES
MLA_TQ = 512


def _mla_up_kernel(cq_ref, ckv_ref, kr_ref, cs_ref, gq_ref, gkv_ref, wq_ref, wkv_ref, q_ref, kn_ref, vt_ref, krd_ref):
    scale = (MLA_NOPE + MLA_ROPE) ** -0.5 * LOG2E
    cs = cs_ref[...]
    q = _dot(_rms(cq_ref[...], gq_ref[...]), wq_ref[...])
    for h in range(MLA_HEADS):
        o = h * MLA_QK
        q_ref[:, o:o + LANES] = (q[:, o:o + LANES] * scale).astype(q_ref.dtype)
        q_ref[:, o + LANES:o + 2 * LANES] = (q[:, o + LANES:o + 2 * LANES] * (cs * scale)).astype(q_ref.dtype)
    kv = _dot(_rms(ckv_ref[...], gkv_ref[...]), wkv_ref[...])
    for h in range(MLA_HEADS):
        o = h * (MLA_NOPE + MLA_V)
        kn_ref[:, h * MLA_NOPE:(h + 1) * MLA_NOPE] = kv[:, o:o + MLA_NOPE].astype(kn_ref.dtype)
        vt_ref[h, 0] = kv[:, o + MLA_NOPE:o + MLA_NOPE + MLA_V].T.astype(vt_ref.dtype)
    krx = kr_ref[...] * cs
    krd_ref[...] = (krx + pltpu.roll(krx, shift=MLA_ROPE, axis=1)).astype(krd_ref.dtype)


def mla_up(z_b, cs, gq, gkv, wq, wkv):
    T = z_b.shape[0]
    tm = MLA_TQ
    nq = wq.shape[1]

    def full(a):
        return pl.BlockSpec(a.shape, lambda i: (0,) * a.ndim)

    return pl.pallas_call(
        _mla_up_kernel,
        grid=(T // tm,),
        in_specs=[
            pl.BlockSpec((tm, MLA_Q_RANK), lambda i: (i, ZB_CQ // MLA_Q_RANK)),
            pl.BlockSpec((tm, MLA_KV_RANK), lambda i: (i, ZB_CKV // MLA_KV_RANK)),
            pl.BlockSpec((tm, LANES), lambda i: (i, ZB_KR // LANES)),
            pl.BlockSpec((tm, LANES), lambda i: (i, 0)),
            full(gq), full(gkv), full(wq), full(wkv),
        ],
        out_specs=[
            pl.BlockSpec((tm, nq), lambda i: (i, 0)),
            pl.BlockSpec((tm, MLA_HEADS * MLA_NOPE), lambda i: (i, 0)),
            pl.BlockSpec((MLA_HEADS, 1, MLA_V, tm), lambda i: (0, i, 0, 0)),
            pl.BlockSpec((tm, LANES), lambda i: (i, 0)),
        ],
        out_shape=[
            jax.ShapeDtypeStruct((T, nq), BF16),
            jax.ShapeDtypeStruct((T, MLA_HEADS * MLA_NOPE), BF16),
            jax.ShapeDtypeStruct((MLA_HEADS, T // tm, MLA_V, tm), BF16),
            jax.ShapeDtypeStruct((T, LANES), BF16),
        ],
        compiler_params=_cparams("parallel"),
        name="mla_up",
    )(z_b, z_b, z_b, cs, gq, gkv, wq, wkv)


def _mla_attn_kernel(q_ref, kn_ref, kr_ref, vt_ref, o_ref, m_ref, l_ref, acc_ref):
    tq = q_ref.shape[0]
    qi = pl.program_id(2)
    q = q_ref[...]
    m_ref[...] = jnp.full_like(m_ref, -jnp.inf)
    l_ref[...] = jnp.zeros_like(l_ref)
    acc_ref[...] = jnp.zeros_like(acc_ref)

    def scores(kb, masked):
        ks = pl.multiple_of(kb * tq, tq)
        kc = jnp.concatenate([kn_ref[pl.ds(ks, tq), :], kr_ref[pl.ds(ks, tq), :]], axis=1)
        s = lax.dot_general(kc, q, (((1,), (1,)), ((), ())), preferred_element_type=F32)
        if masked:
            row = lax.broadcasted_iota(jnp.int32, s.shape, 0)
            col = lax.broadcasted_iota(jnp.int32, s.shape, 1)
            s = jnp.where(row <= col, s, MASK_VALUE)
        return s

    def blocks(kbs, masked):
        s = [scores(kb, masked) for kb in kbs]
        m_old = m_ref[...]
        m_new = m_old
        for si in s:
            m_new = jnp.maximum(m_new, jnp.max(si, axis=0, keepdims=True))
        alpha = jnp.exp2(m_old - m_new)
        p = [jnp.exp2(si - m_new) for si in s]
        l_new = alpha * l_ref[...]
        acc = alpha * acc_ref[...]
        for kb, pi in zip(kbs, p):
            l_new = l_new + jnp.sum(pi, axis=0, keepdims=True)
            acc = acc + jnp.dot(vt_ref[0, kb], pi.astype(BF16), preferred_element_type=F32)
        l_ref[...] = l_new
        acc_ref[...] = acc
        m_ref[...] = m_new

    def body(i, carry):
        blocks([2 * i, 2 * i + 1], False)
        return carry

    lax.fori_loop(0, qi // 2, body, 0)

    @pl.when(qi % 2 == 1)
    def _():
        blocks([qi - 1], False)

    blocks([qi], True)
    o_ref[...] = (acc_ref[...] / l_ref[...]).T.astype(o_ref.dtype)


def mla_attn(q, kn, vt, krd, batch, seq):
    T = q.shape[0]
    tq = MLA_TQ
    nq = seq // tq
    return pl.pallas_call(
        _mla_attn_kernel,
        grid=(batch, MLA_HEADS, nq),
        in_specs=[
            pl.BlockSpec((tq, MLA_QK), lambda b, h, i: (b * nq + i, h)),
            pl.BlockSpec((seq, MLA_NOPE), lambda b, h, i: (b, h)),
            pl.BlockSpec((seq, LANES), lambda b, h, i: (b, 0)),
            pl.BlockSpec((1, nq, MLA_V, tq), lambda b, h, i: (h, b, 0, 0)),
        ],
        out_specs=pl.BlockSpec((tq, MLA_V), lambda b, h, i: (b * nq + i, h)),
        out_shape=jax.ShapeDtypeStruct((T, MLA_HEADS * MLA_V), BF16),
        scratch_shapes=[pltpu.VMEM((1, tq), F32), pltpu.VMEM((1, tq), F32), pltpu.VMEM((MLA_V, tq), F32)],
        compiler_params=_cparams("parallel", "parallel", "arbitrary"),
        name="mla_attn",
    )(q, kn, krd, vt)


def _merge_kernel(x_ref, ohg_ref, orw_ref, omla_ref, g0_ref, g1_ref, g2_ref, wb_ref, wo_ref, gp_ref, o_ref):
    merged = (_sigmoid(g0_ref[...]) * jnp.dot(ohg_ref[...], wb_ref[0], preferred_element_type=F32)
              + _sigmoid(g1_ref[...]) * jnp.dot(orw_ref[...], wb_ref[1], preferred_element_type=F32)
              + _sigmoid(g2_ref[...]) * jnp.dot(omla_ref[...], wb_ref[2], preferred_element_type=F32))
    y = _dot(merged, wo_ref[...])
    o_ref[...] = x_ref[...] + _rms(y, gp_ref[...])


def merge(x, o_hg, o_rw, o_mla, z_a, w_branch, w_out, g_post):
    T, D = x.shape
    tm = MERGE_ROWS
    row = pl.BlockSpec((tm, D), lambda i: (i, 0))

    def gate(n):
        return pl.BlockSpec((tm, D), lambda i, n=n: (i, ZA_GATE // D + n))

    def full(a):
        return pl.BlockSpec(a.shape, lambda i: (0,) * a.ndim)

    return pl.pallas_call(
        _merge_kernel,
        grid=(T // tm,),
        in_specs=[row, row, row, row, gate(0), gate(1), gate(2), full(w_branch), full(w_out), full(g_post)],
        out_specs=row,
        out_shape=jax.ShapeDtypeStruct((T, D), F32),
        compiler_params=_cparams("parallel"),
        name="merge",
    )(x, o_hg, o_rw, o_mla, z_a, z_a, z_a, w_branch, w_out, g_post)


FFN_TILE = 256


def _ffn_kernel(x_ref, gpre_ref, wg_ref, wu_ref, wo_ref, gpost_ref, o_ref, h_ref, acc_ref):
    j = pl.program_id(1)

    @pl.when(j == 0)
    def _():
        h_ref[...] = _rms(x_ref[...], gpre_ref[...]).astype(BF16)
        acc_ref[...] = jnp.zeros_like(acc_ref)

    h = h_ref[...]
    gate = jnp.dot(h, wg_ref[...], preferred_element_type=F32)
    up = jnp.dot(h, wu_ref[...], preferred_element_type=F32)
    act = gate * _sigmoid(gate) * up
    acc_ref[...] += _dot(act, wo_ref[...])

    @pl.when(j == pl.num_programs(1) - 1)
    def _():
        o_ref[...] = x_ref[...] + _rms(acc_ref[...], gpost_ref[...])


def ffn(x, g_pre, w_in, w_out, g_post):
    T, D = x.shape
    tm = FFN_ROWS
    tf = FFN_TILE
    nf = D_FF // tf
    return pl.pallas_call(
        _ffn_kernel,
        grid=(T // tm, nf),
        in_specs=[
            pl.BlockSpec((tm, D), lambda i, j: (i, 0)),
            pl.BlockSpec((1, D), lambda i, j: (0, 0)),
            pl.BlockSpec((D, tf), lambda i, j: (0, j)),
            pl.BlockSpec((D, tf), lambda i, j: (0, nf + j)),
            pl.BlockSpec((tf, D), lambda i, j: (j, 0)),
            pl.BlockSpec((1, D), lambda i, j: (0, 0)),
        ],
        out_specs=pl.BlockSpec((tm, D), lambda i, j: (i, 0)),
        out_shape=jax.ShapeDtypeStruct((T, D), F32),
        scratch_shapes=[pltpu.VMEM((tm, D), BF16), pltpu.VMEM((tm, D), F32)],
        compiler_params=_cparams("parallel", "arbitrary"),
        name="ffn",
    )(x, g_pre, w_in, w_in, w_out, g_post)


def _swap_halves(w):
    h = w.shape[-1] // 2
    return jnp.concatenate([w[..., h:], w[..., :h]], axis=-1)


def _pad_rows(w, rows, at):
    out = jnp.zeros((rows, w.shape[1]), w.dtype)
    return out.at[at:at + w.shape[0]].set(w)


def _pack_in_proj(w, w_vres):
    hg, rkv, wa, g, cq, ckv, kr, gates = jnp.split(
        w, np.cumsum([4096, 3072, RW_LORA_W + RW_LORA_A, RW_LORA_G, MLA_Q_RANK, MLA_KV_RANK, MLA_ROPE]).tolist(),
        axis=1)
    wide = jnp.concatenate([hg, rkv, gates], axis=1)
    parts = [ckv, kr, _swap_halves(kr), cq, wa, g]
    if w_vres is not None:
        parts.append(jnp.pad(w_vres, ((0, 0), (0, LANES - RW_LORA_V))))
    narrow = jnp.concatenate(parts, axis=1)
    return wide.astype(BF16), narrow.astype(BF16)


def _pack_wq(w_uq):
    w = w_uq.reshape(MLA_Q_RANK, MLA_HEADS, MLA_NOPE + MLA_ROPE)
    rope = w[..., MLA_NOPE:]
    w = jnp.concatenate([w[..., :MLA_NOPE], rope, _swap_halves(rope)], axis=-1)
    return w.reshape(MLA_Q_RANK, MLA_HEADS * MLA_QK).astype(BF16)


def _row(v):
    return v.reshape(1, -1).astype(F32)


def kernel(x, positions, hgrn_lb_logits, mix_pre_g, mix_post_g, ffn_pre_g, ffn_post_g, w_in, w_vres_down,
           hgrn_onorm_g, rwkv_mu, rwkv_vres_mu, rwkv_w0, rwkv_w_up, rwkv_a0, rwkv_a_up, rwkv_g_up, rwkv_v0,
           rwkv_v_up, rwkv_k_k, rwkv_k_a, rwkv_r_k, rwkv_lnx_g, rwkv_lnx_b, mla_q_norm_g, mla_w_uq,
           mla_kv_norm_g, mla_w_ukv, w_branch, w_out, w_ffn_in, w_ffn_out):
    batch, seq, d = x.shape
    depth = w_in.shape[0]
    T = batch * seq
    xt = x.reshape(T, d)

    probs = jax.nn.softmax(hgrn_lb_logits.astype(F32), axis=0)
    lower_bounds = jnp.cumsum(probs, axis=0) - probs[0]
    inv_freq = ROPE_THETA ** (-jnp.arange(0, MLA_ROPE, 2, dtype=F32) / MLA_ROPE)
    ang = positions.astype(F32).reshape(T, 1) * inv_freq
    cos, sin = jnp.cos(ang), jnp.sin(ang)
    cs = jnp.concatenate([cos, cos, -sin, sin], axis=-1)

    W = RW_WIDTH
    v_first = None
    for l in range(depth):
        w_wide, w_narrow = _pack_in_proj(w_in[l], None if l == 0 else w_vres_down[l - 1])
        g_pre = _row(mix_pre_g[l])
        z_a = norm_matmul(xt, g_pre, w_wide, PROJ_COLS)
        z_b = norm_matmul(xt, g_pre, w_narrow, w_narrow.shape[1])

        o_hg = hgrn(z_a, _row(lower_bounds[l]), _row(hgrn_onorm_g[l]), batch, seq)

        mu = rwkv_mu[l]
        p = {
            "mu_r": _row(mu[:W]), "mu_k": _row(mu[W:2 * W]), "mu_v": _row(mu[2 * W:3 * W]),
            "mu_wa": _row(mu[3 * W:3 * W + LANES]), "mu_g": _row(mu[3 * W + LANES:]),
            "w0": _row(rwkv_w0[l]), "w_up": _pad_rows(rwkv_w_up[l], LANES, 0).astype(BF16),
            "a0": _row(rwkv_a0[l]), "a_up": _pad_rows(rwkv_a_up[l], LANES, RW_LORA_W).astype(BF16),
            "g_up": rwkv_g_up[l].astype(BF16),
            "k_k": _row(rwkv_k_k[l]), "k_a": _row(rwkv_k_a[l]),
        }
        if l > 0:
            p["mu_zv"] = _row(jnp.pad(rwkv_vres_mu[l - 1], (0, LANES - RW_LORA_V)))
            p["v0"] = _row(rwkv_v0[l - 1])
            p["v_up"] = _pad_rows(rwkv_v_up[l - 1], LANES, 0).astype(BF16)
        r, ld, k, v, kk, a, g = rw_prep(z_a, z_b, v_first, p, seq)
        if l == 0:
            v_first = v
        o_rw = rw_chunk(r, ld, k, v, kk, a, g, _row(rwkv_r_k[l]), _row(rwkv_lnx_g[l]), _row(rwkv_lnx_b[l]),
                        batch, seq)

        q, kn, vt, krd = mla_up(z_b, cs, _row(mla_q_norm_g[l]), _row(mla_kv_norm_g[l]), _pack_wq(mla_w_uq[l]),
                                mla_w_ukv[l].astype(BF16))
        o_mla = mla_attn(q, kn, vt, krd, batch, seq)

        xt = merge(xt, o_hg, o_rw, o_mla, z_a, w_branch[l].astype(BF16), w_out[l].astype(BF16),
                   _row(mix_post_g[l]))
        xt = ffn(xt, _row(ffn_pre_g[l]), w_ffn_in[l].astype(BF16), w_ffn_out[l].astype(BF16),
                 _row(ffn_post_g[l]))
    return xt.reshape(batch, seq, d)
```

```python
import functools
import math

import jax
import jax.numpy as jnp
import numpy as np
from jax import lax
from jax.experimental import pallas as pl
from jax.experimental.pallas import tpu as pltpu

F32 = jnp.float32
BF16 = jnp.bfloat16
LOG2E = math.log2(math.e)

D_MODEL = 1024
NORM_EPS = 1e-6
MASK_VALUE = -1e30
MIN_GATE = 1e-30
HG_HEADS = 8
HG_DK = 128
RW_HEADS = 16
RW_N = 64
RW_WIDTH = RW_HEADS * RW_N
RW_LORA_W = 64
RW_LORA_A = 64
RW_LORA_G = 128
RW_LORA_V = 32
RW_LNX_EPS = 1e-5 * RW_N
MLA_HEADS = 8
MLA_Q_RANK = 384
MLA_KV_RANK = 256
MLA_NOPE = 128
MLA_ROPE = 64
MLA_V = 128
ROPE_THETA = 10000.0
D_FF = 2816

LANES = 128
SUBLANES = 8
VMEM_LIMIT = 48 * 1024 * 1024

PROJ_ROWS = 1024
PROJ_COLS = 2048
RW_PREP_ROWS = 256
MLA_UP_ROWS = 512
MERGE_ROWS = 256
FFN_ROWS = 1024

ZA_HG, ZA_RKV, ZA_GATE, ZA_WIDTH = 0, 4096, 7168, 10240
ZB_CKV, ZB_KR, ZB_CQ, ZB_WA, ZB_G, ZB_VRES = 0, 256, 384, 768, 896, 1024


def _cparams(*sem):
    return pltpu.CompilerParams(dimension_semantics=sem, vmem_limit_bytes=VMEM_LIMIT)


def _sigmoid(x):
    return 1.0 / (1.0 + jnp.exp(-x))


def _rms(x, g):
    ms = jnp.mean(x * x, axis=-1, keepdims=True)
    return x * lax.rsqrt(ms + NORM_EPS) * g


def _dot(a, b):
    return jnp.dot(a.astype(BF16), b.astype(BF16), preferred_element_type=F32)


def _dot_nt(a, b):
    return lax.dot_general(a.astype(BF16), b.astype(BF16), (((1,), (1,)), ((), ())), preferred_element_type=F32)


def _dot_tn(a, b):
    return lax.dot_general(a.astype(BF16), b.astype(BF16), (((0,), (0,)), ((), ())), preferred_element_type=F32)


def _tril_ones(n):
    row = lax.broadcasted_iota(jnp.int32, (n, n), 0)
    col = lax.broadcasted_iota(jnp.int32, (n, n), 1)
    return jnp.where(col <= row, 1.0, 0.0).astype(BF16)


def _cumsum_rows(tril, x):
    n = x.shape[1]
    hi = x.astype(BF16)
    r1 = x - hi.astype(F32)
    mid = r1.astype(BF16)
    lo = (r1 - mid.astype(F32)).astype(BF16)
    y = jnp.dot(tril, jnp.concatenate([hi, mid, lo], axis=1), preferred_element_type=F32)
    return y[:, :n] + y[:, n:2 * n] + y[:, 2 * n:]


def _norm_matmul_kernel(x_ref, g_ref, w_ref, o_ref, h_ref):
    @pl.when(pl.program_id(1) == 0)
    def _():
        h_ref[...] = _rms(x_ref[...], g_ref[...]).astype(BF16)

    o_ref[...] = jnp.dot(h_ref[...], w_ref[...], preferred_element_type=F32).astype(o_ref.dtype)


def norm_matmul(x, g, w, tn):
    T, D = x.shape
    N = w.shape[1]
    tm = PROJ_ROWS
    return pl.pallas_call(
        _norm_matmul_kernel,
        grid=(T // tm, N // tn),
        in_specs=[
            pl.BlockSpec((tm, D), lambda i, j: (i, 0)),
            pl.BlockSpec((1, D), lambda i, j: (0, 0)),
            pl.BlockSpec((D, tn), lambda i, j: (0, j)),
        ],
        out_specs=pl.BlockSpec((tm, tn), lambda i, j: (i, j)),
        out_shape=jax.ShapeDtypeStruct((T, N), F32),
        scratch_shapes=[pltpu.VMEM((tm, D), BF16)],
        compiler_params=_cparams("parallel", "arbitrary"),
        name="norm_matmul",
    )(x, g, w)


HG_CHUNK = 128


def _hgrn_term_codes(C):
    t = np.arange(C)[:, None]
    s = np.arange(C)[None, :]
    codes = np.full((C, C), -1, np.int32)
    same8 = (t // SUBLANES == s // SUBLANES) & (s <= t)
    codes = np.where(same8, s % SUBLANES, codes)
    m, i = SUBLANES, 0
    while m < C:
        hit = ((t // m) % 2 == 1) & (s // m == t // m - 1)
        codes = np.where(hit, SUBLANES + i, codes)
        m, i = 2 * m, i + 1
    return codes.astype(np.int32)


def _hgrn_kernel(q_ref, f_ref, v_ref, og_ref, lb_ref, gn_ref, code_ref, o_ref, st_ref):
    C = q_ref.shape[0]
    heads = range(q_ref.shape[1] // HG_DK)

    @pl.when(pl.program_id(1) == 0)
    def _():
        st_ref[...] = jnp.zeros_like(st_ref)

    def hs(ref, h):
        return ref[:, h * HG_DK:(h + 1) * HG_DK]

    code = code_ref[...]
    tril = _tril_ones(C)
    ones = jnp.ones((HG_DK, C), BF16)
    g8 = C // SUBLANES

    q = [hs(q_ref, h) for h in heads]
    zf = [hs(f_ref, h) for h in heads]
    v = [hs(v_ref, h) for h in heads]
    lb = [hs(lb_ref, h) for h in heads]
    lf = [jnp.log2(jnp.maximum(lb[h] + (1.0 - lb[h]) * _sigmoid(zf[h]), MIN_GATE)) for h in heads]
    k = [(1.0 - lb[h]) * _sigmoid(-zf[h]) for h in heads]
    b = [_cumsum_rows(tril, lf[h]) for h in heads]

    scores = [jnp.zeros((C, C), F32) for _ in heads]
    b3 = [b[h].reshape(g8, SUBLANES, HG_DK) for h in heads]
    k3 = [k[h].reshape(g8, SUBLANES, HG_DK) for h in heads]
    for j in range(SUBLANES):
        bj = [jnp.broadcast_to(b3[h][:, j:j + 1, :], (g8, SUBLANES, HG_DK)).reshape(C, HG_DK) for h in heads]
        kj = [jnp.broadcast_to(k3[h][:, j:j + 1, :], (g8, SUBLANES, HG_DK)).reshape(C, HG_DK) for h in heads]
        xj = [(q[h] * kj[h] * jnp.exp2(jnp.minimum(b[h] - bj[h], 0.0))).astype(BF16) for h in heads]
        cj = [jnp.dot(xj[h], ones, preferred_element_type=F32) for h in heads]
        scores = [jnp.where(code == j, cj[h], scores[h]) for h in heads]
    rowc = lax.broadcasted_iota(jnp.int32, (C, HG_DK), 0)
    m, i = SUBLANES, 0
    while m < C:
        odd = ((rowc >> (m.bit_length() - 1)) & 1) == 1
        xm = []
        for h in heads:
            pieces = [jnp.broadcast_to(b[h][r0:r0 + 1, :], (2 * m, HG_DK)) for r0 in range(m - 1, C, 2 * m)]
            ref = pieces[0] if len(pieces) == 1 else jnp.concatenate(pieces, axis=0)
            d = b[h] - ref
            xm.append((jnp.where(odd, q[h], k[h]) * jnp.exp2(jnp.minimum(d, -d))).astype(BF16))
        gm = [_dot_nt(xm[h], xm[h]) for h in heads]
        scores = [jnp.where(code == SUBLANES + i, gm[h], scores[h]) for h in heads]
        m, i = 2 * m, i + 1

    st = [st_ref[h] for h in heads]
    o = [_dot(scores[h], v[h]) + _dot_nt(q[h] * jnp.exp2(b[h]), st[h]) for h in heads]
    bl = [b[h][C - 1:C, :] for h in heads]
    for h in heads:
        st_ref[h] = st[h] * jnp.exp2(bl[h]) + _dot_tn(v[h], k[h] * jnp.exp2(bl[h] - b[h]))
    for h in heads:
        og = hs(og_ref, h)
        o_ref[:, h * HG_DK:(h + 1) * HG_DK] = (_rms(o[h], gn_ref[...]) * (og * _sigmoid(og))).astype(o_ref.dtype)


def hgrn(z_a, lower_bound, onorm_g, batch, seq):
    T = z_a.shape[0]
    C = HG_CHUNK
    nc = seq // C
    W = HG_HEADS * HG_DK
    codes = jnp.asarray(_hgrn_term_codes(C))

    def zspec(seg):
        return pl.BlockSpec((C, W), lambda b, c, seg=seg: (b * nc + c, ZA_HG // W + seg))

    return pl.pallas_call(
        _hgrn_kernel,
        grid=(batch, nc),
        in_specs=[
            zspec(0), zspec(1), zspec(2), zspec(3),
            pl.BlockSpec((1, W), lambda b, c: (0, 0)),
            pl.BlockSpec((1, HG_DK), lambda b, c: (0, 0)),
            pl.BlockSpec((C, C), lambda b, c: (0, 0)),
        ],
        out_specs=pl.BlockSpec((C, W), lambda b, c: (b * nc + c, 0)),
        out_shape=jax.ShapeDtypeStruct((T, W), BF16),
        scratch_shapes=[pltpu.VMEM((HG_HEADS, HG_DK, HG_DK), F32)],
        compiler_params=_cparams("parallel", "arbitrary"),
        name="hgrn",
    )(z_a, z_a, z_a, z_a, lower_bound, onorm_g, codes)


def _head_seg():
    r = lax.broadcasted_iota(jnp.int32, (LANES, LANES), 0)
    c = lax.broadcasted_iota(jnp.int32, (LANES, LANES), 1)
    lg = RW_N.bit_length() - 1
    seg = jnp.where((r >> lg) == (c >> lg), 1.0, 0.0).astype(BF16)
    return jnp.concatenate([seg, seg], axis=0)


def _head_sum(x, seg):
    hi = x.astype(BF16)
    lo = (x - hi.astype(F32)).astype(BF16)
    return jnp.dot(jnp.concatenate([hi, lo], axis=1), seg, preferred_element_type=F32)


def _shift_mix(cur_ref, prev_ref, mu, first):
    z = cur_ref[...]
    last = prev_ref[SUBLANES - 1:SUBLANES, :]
    last = jnp.where(first, jnp.zeros_like(last), last)
    rolled = pltpu.roll(z, shift=1, axis=0)
    rid = lax.broadcasted_iota(jnp.int32, z.shape, 0)
    zp = jnp.where(rid == 0, jnp.broadcast_to(last, z.shape), rolled)
    return z + (zp - z) * mu


def _rw_prep_kernel(has_vres, seq, *refs):
    if has_vres:
        (r_ref, rp_ref, k_ref, kp_ref, v_ref, vp_ref, wa_ref, wap_ref, g_ref, gp_ref, zv_ref, zvp_ref, vf_ref,
         mu_r, mu_k, mu_v, mu_wa, mu_g, mu_zv, w0, w_up, a0, a_up, g_up, v0, v_up, k_k, k_a,
         ro, ldo, ko, vo, kko, ao, go) = refs
    else:
        (r_ref, rp_ref, k_ref, kp_ref, v_ref, vp_ref, wa_ref, wap_ref, g_ref, gp_ref,
         mu_r, mu_k, mu_v, mu_wa, mu_g, w0, w_up, a0, a_up, g_up, k_k, k_a,
         ro, ldo, ko, vo, kko, ao, go) = refs
    tm = r_ref.shape[0]
    first = (pl.program_id(0) * tm) % seq == 0

    r = _shift_mix(r_ref, rp_ref, mu_r[...], first)
    k = _shift_mix(k_ref, kp_ref, mu_k[...], first)
    v = _shift_mix(v_ref, vp_ref, mu_v[...], first)
    wa = _shift_mix(wa_ref, wap_ref, mu_wa[...], first)
    zg = _shift_mix(g_ref, gp_ref, mu_g[...], first)

    zw = w0[...] + _dot(jnp.tanh(wa), w_up[...])
    w_log = -(jnp.maximum(-zw, 0.0) + jnp.log(1.0 + jnp.exp(-jnp.abs(zw)))) - 0.5
    ldo[...] = -jnp.exp(w_log)
    a = _sigmoid(a0[...] + _dot(wa, a_up[...]))
    go[...] = _dot(_sigmoid(zg), g_up[...]).astype(go.dtype)
    if has_vres:
        zv = _shift_mix(zv_ref, zvp_ref, mu_zv[...], first)
        v = v + (vf_ref[...].astype(F32) - v) * _sigmoid(v0[...] + _dot(zv, v_up[...]))
    kk = k * k_k[...]
    seg = _head_seg()
    for p in range(RW_WIDTH // LANES):
        sl = slice(p * LANES, (p + 1) * LANES)
        kkp = kk[:, sl]
        ss = _head_sum(kkp * kkp, seg)
        kko[:, sl] = (kkp * lax.rsqrt(jnp.maximum(ss, 1e-24))).astype(kko.dtype)
    ro[...] = r.astype(ro.dtype)
    ko[...] = (k * (1.0 + (a - 1.0) * k_a[...])).astype(ko.dtype)
    vo[...] = v.astype(vo.dtype)
    ao[...] = a.astype(ao.dtype)


def rw_prep(z_a, z_b, v_first, p, seq):
    T = z_a.shape[0]
    has_vres = v_first is not None
    W = RW_WIDTH
    tm = RW_PREP_ROWS
    tb = tm // SUBLANES

    def cur(width, off):
        return pl.BlockSpec((tm, width), lambda i: (i, off // width))

    def prev(width, off):
        return pl.BlockSpec((SUBLANES, width), lambda i: (jnp.maximum(i * tb - 1, 0), off // width))

    def full(a):
        return pl.BlockSpec(a.shape, lambda i: (0,) * a.ndim)

    acts = [z_a, z_a, z_a, z_a, z_a, z_a, z_b, z_b, z_b, z_b]
    specs = [cur(W, ZA_RKV), prev(W, ZA_RKV), cur(W, ZA_RKV + W), prev(W, ZA_RKV + W),
             cur(W, ZA_RKV + 2 * W), prev(W, ZA_RKV + 2 * W),
             cur(LANES, ZB_WA), prev(LANES, ZB_WA), cur(LANES, ZB_G), prev(LANES, ZB_G)]
    if has_vres:
        acts += [z_b, z_b, v_first]
        specs += [cur(LANES, ZB_VRES), prev(LANES, ZB_VRES), pl.BlockSpec((tm, W), lambda i: (i, 0))]
        names = ["mu_r", "mu_k", "mu_v", "mu_wa", "mu_g", "mu_zv", "w0", "w_up", "a0", "a_up", "g_up", "v0", "v_up",
                 "k_k", "k_a"]
    else:
        names = ["mu_r", "mu_k", "mu_v", "mu_wa", "mu_g", "w0", "w_up", "a0", "a_up", "g_up", "k_k", "k_a"]
    params = [p[n] for n in names]
    out_spec = pl.BlockSpec((tm, W), lambda i: (i, 0))
    return pl.pallas_call(
        functools.partial(_rw_prep_kernel, has_vres, seq),
        grid=(T // tm,),
        in_specs=specs + [full(a) for a in params],
        out_specs=[out_spec] * 7,
        out_shape=[jax.ShapeDtypeStruct((T, W), F32 if i == 1 else BF16) for i in range(7)],
        compiler_params=_cparams("parallel"),
        name="rw_prep",
    )(*acts, *params)


RW_CHUNK = 64


def _rw_chunk_kernel(r_ref, ld_ref, k_ref, v_ref, kk_ref, a_ref, g_ref, rk_ref, lg_ref, lb_ref, o_ref, st_ref):
    C = r_ref.shape[0]
    C2 = 2 * C
    pairs = range(r_ref.shape[1] // LANES)

    @pl.when(pl.program_id(1) == 0)
    def _():
        st_ref[...] = jnp.zeros_like(st_ref)

    def ps(ref, p):
        return ref[:, p * LANES:(p + 1) * LANES].astype(F32)

    tril = _tril_ones(C)
    seg = _head_seg()
    lane = lax.broadcasted_iota(jnp.int32, (C, LANES), 1)
    row = lax.broadcasted_iota(jnp.int32, (C2, C2), 0)
    col = lax.broadcasted_iota(jnp.int32, (C2, C2), 1)
    strict = (col & (C - 1)) < (row & (C - 1))
    incl = (col & (C - 1)) <= (row & (C - 1))
    eye = jnp.where(row == col, 1.0, 0.0)

    def stack(x):
        return jnp.concatenate([jnp.where(lane < RW_N, x, 0.0), jnp.where(lane >= RW_N, x, 0.0)], axis=0).astype(BF16)

    r = [ps(r_ref, p) for p in pairs]
    ld = [ps(ld_ref, p) for p in pairs]
    k = [ps(k_ref, p) for p in pairs]
    v = [ps(v_ref, p) for p in pairs]
    kk = [ps(kk_ref, p) for p in pairs]
    kka = [kk[p] * ps(a_ref, p) for p in pairs]
    cl_all = _cumsum_rows(tril, ld_ref[...])
    cl = [cl_all[:, p * LANES:(p + 1) * LANES] for p in pairs]
    gc = [cl[p][C - 1:C, :] for p in pairs]
    e_inv = [jnp.exp(-cl[p]) for p in pairs]
    e_end = [jnp.exp(gc[p] - cl[p]) for p in pairs]
    ar = [jnp.concatenate([stack(-kk[p] * jnp.exp(cl[p] - ld[p])), stack(r[p] * jnp.exp(cl[p]))], axis=0)
          for p in pairs]
    bk = [jnp.concatenate([stack(kka[p] * e_inv[p]), stack(k[p] * e_inv[p])], axis=0) for p in pairs]
    bkg = [jnp.concatenate([stack(kka[p] * e_end[p]), stack(k[p] * e_end[p])], axis=0) for p in pairs]
    vv = [stack(v[p]) for p in pairs]

    g4 = [_dot_nt(ar[p], bk[p]) for p in pairs]
    lab = [jnp.where(strict, g4[p][:C2, :C2], 0.0) for p in pairs]
    lkm = [jnp.concatenate([jnp.where(strict, g4[p][:C2, C2:], 0.0), jnp.where(incl, g4[p][C2:, C2:], 0.0)],
                           axis=0).astype(BF16) for p in pairs]
    mb = [jnp.where(incl, g4[p][C2:, :C2], 0.0).astype(BF16) for p in pairs]
    qi = [eye + lab[p] for p in pairs]
    mp = [_dot(lab[p], lab[p]) for p in pairs]
    n = 2
    while 2 * n < C:
        mq = [_dot(mp[p], jnp.concatenate([mp[p].astype(BF16), qi[p].astype(BF16)], axis=1)) for p in pairs]
        mp = [mq[p][:, :C2] for p in pairs]
        qi = [qi[p] + mq[p][:, C2:] for p in pairs]
        n *= 2
    qi = [qi[p] + _dot(mp[p], qi[p]) for p in pairs]

    zt = [st_ref[p] for p in pairs]
    fz = [_dot(jnp.concatenate([ar[p], lkm[p]], axis=1), jnp.concatenate([zt[p].T.astype(BF16), vv[p]], axis=0))
          for p in pairs]
    uu = [_dot(qi[p], fz[p][:C2]) for p in pairs]
    yy = [fz[p][C2:] + _dot(mb[p], uu[p]) for p in pairs]
    for p in pairs:
        uv = jnp.concatenate([uu[p].astype(BF16), vv[p]], axis=0)
        st_ref[p] = zt[p] * jnp.exp(gc[p]) + _dot_tn(uv, bkg[p])

    y = [yy[p][:C] + yy[p][C:] for p in pairs]
    mean = [_head_sum(y[p], seg) * (1.0 / RW_N) for p in pairs]
    d = [y[p] - mean[p] for p in pairs]
    var = [_head_sum(d[p] * d[p], seg) * (1.0 / RW_N) for p in pairs]
    bonus = [_head_sum(r[p] * k[p] * ps(rk_ref, p), seg) * v[p] for p in pairs]
    for p in pairs:
        yn = d[p] * lax.rsqrt(var[p] + RW_LNX_EPS) * ps(lg_ref, p) + ps(lb_ref, p)
        o_ref[:, p * LANES:(p + 1) * LANES] = ((yn + bonus[p]) * ps(g_ref, p)).astype(o_ref.dtype)


def rw_chunk(r, ld, k, v, kk, a, g, r_k, lnx_g, lnx_b, batch, seq):
    T, W = r.shape
    C = RW_CHUNK
    nc = seq // C
    act = pl.BlockSpec((C, W), lambda b, c: (b * nc + c, 0))
    par = pl.BlockSpec((1, W), lambda b, c: (0, 0))
    return pl.pallas_call(
        _rw_chunk_kernel,
        grid=(batch, nc),
        in_specs=[act] * 7 + [par] * 3,
        out_specs=act,
        out_shape=jax.ShapeDtypeStruct((T, W), BF16),
        scratch_shapes=[pltpu.VMEM((W // LANES, LANES, LANES), F32)],
        compiler_params=_cparams("parallel", "arbitrary"),
        name="rw_chunk",
    )(r, ld, k, v, kk, a, g, r_k, lnx_g, lnx_b)


MLA_QK = 2 * LANES
MLA_TQ = 512
MLA_VT_ROWS = MLA_V + 16


def _mla_up_kernel(cq_ref, ckv_ref, kr_ref, cs_ref, gq_ref, gkv_ref, wq_ref, wkv_ref, q_ref, kn_ref, vt_ref, krd_ref):
    scale = (MLA_NOPE + MLA_ROPE) ** -0.5 * LOG2E
    cs = cs_ref[...]
    q = _dot(_rms(cq_ref[...], gq_ref[...]), wq_ref[...])
    for h in range(MLA_HEADS):
        o = h * MLA_QK
        q_ref[:, o:o + LANES] = (q[:, o:o + LANES] * scale).astype(q_ref.dtype)
        q_ref[:, o + LANES:o + 2 * LANES] = (q[:, o + LANES:o + 2 * LANES] * (cs * scale)).astype(q_ref.dtype)
    kv = _dot(_rms(ckv_ref[...], gkv_ref[...]), wkv_ref[...])
    for h in range(MLA_HEADS):
        o = h * (MLA_NOPE + MLA_V)
        kn_ref[:, h * MLA_NOPE:(h + 1) * MLA_NOPE] = kv[:, o:o + MLA_NOPE].astype(kn_ref.dtype)
        vt_ref[h, 0, :MLA_V, :] = kv[:, o + MLA_NOPE:o + MLA_NOPE + MLA_V].T.astype(vt_ref.dtype)
        vt_ref[h, 0, MLA_V:, :] = jnp.ones((MLA_VT_ROWS - MLA_V, kv.shape[0]), vt_ref.dtype)
    krx = kr_ref[...] * cs
    krd_ref[...] = (krx + pltpu.roll(krx, shift=MLA_ROPE, axis=1)).astype(krd_ref.dtype)


def mla_up(z_b, cs, gq, gkv, wq, wkv):
    T = z_b.shape[0]
    tm = MLA_TQ
    nq = wq.shape[1]

    def full(a):
        return pl.BlockSpec(a.shape, lambda i: (0,) * a.ndim)

    return pl.pallas_call(
        _mla_up_kernel,
        grid=(T // tm,),
        in_specs=[
            pl.BlockSpec((tm, MLA_Q_RANK), lambda i: (i, ZB_CQ // MLA_Q_RANK)),
            pl.BlockSpec((tm, MLA_KV_RANK), lambda i: (i, ZB_CKV // MLA_KV_RANK)),
            pl.BlockSpec((tm, LANES), lambda i: (i, ZB_KR // LANES)),
            pl.BlockSpec((tm, LANES), lambda i: (i, 0)),
            full(gq), full(gkv), full(wq), full(wkv),
        ],
        out_specs=[
            pl.BlockSpec((tm, nq), lambda i: (i, 0)),
            pl.BlockSpec((tm, MLA_HEADS * MLA_NOPE), lambda i: (i, 0)),
            pl.BlockSpec((MLA_HEADS, 1, MLA_VT_ROWS, tm), lambda i: (0, i, 0, 0)),
            pl.BlockSpec((tm, LANES), lambda i: (i, 0)),
        ],
        out_shape=[
            jax.ShapeDtypeStruct((T, nq), BF16),
            jax.ShapeDtypeStruct((T, MLA_HEADS * MLA_NOPE), BF16),
            jax.ShapeDtypeStruct((MLA_HEADS, T // tm, MLA_VT_ROWS, tm), BF16),
            jax.ShapeDtypeStruct((T, LANES), BF16),
        ],
        compiler_params=_cparams("parallel"),
        name="mla_up",
    )(z_b, z_b, z_b, cs, gq, gkv, wq, wkv)


def _mla_attn_kernel(q_ref, kn_ref, kr_ref, vt_ref, o_ref, m_ref, acc_ref, s_ref):
    tq = q_ref.shape[0]
    qi = pl.program_id(2)
    q = q_ref[...]
    m_ref[...] = jnp.full_like(m_ref, -jnp.inf)
    acc_ref[...] = jnp.zeros_like(acc_ref)
    row = lax.broadcasted_iota(jnp.int32, (tq, tq), 0)
    col = lax.broadcasted_iota(jnp.int32, (tq, tq), 1)
    causal = row <= col

    def scores(kb):
        ks = pl.multiple_of(kb * tq, tq)
        kc = jnp.concatenate([kn_ref[pl.ds(ks, tq), :], kr_ref[pl.ds(ks, tq), :]], axis=1)
        return lax.dot_general(kc, q, (((1,), (1,)), ((), ())), preferred_element_type=F32)

    def update(kb):
        s = s_ref[...]
        m_old = m_ref[...]
        m_new = jnp.maximum(m_old, jnp.max(s, axis=0, keepdims=True))
        p = jnp.exp2(s - m_new).astype(BF16)
        acc_ref[...] = jnp.exp2(m_old - m_new) * acc_ref[...] + jnp.dot(vt_ref[0, kb], p, preferred_element_type=F32)
        m_ref[...] = m_new

    s0 = scores(0)
    s_ref[...] = jnp.where(qi > 0, s0, jnp.where(causal, s0, MASK_VALUE))

    def body(kb, carry):
        s_next = scores(kb + 1)
        update(kb)
        s_ref[...] = s_next
        return carry

    lax.fori_loop(0, qi - 1, body, 0)

    @pl.when(qi > 0)
    def _():
        s_next = jnp.where(causal, scores(qi), MASK_VALUE)
        update(qi - 1)
        s_ref[...] = s_next

    update(qi)
    acc = acc_ref[...]
    o_ref[...] = (acc[:MLA_V] / acc[MLA_V:MLA_V + 1]).T.astype(o_ref.dtype)


def mla_attn(q, kn, vt, krd, batch, seq):
    T = q.shape[0]
    tq = MLA_TQ
    nq = seq // tq
    return pl.pallas_call(
        _mla_attn_kernel,
        grid=(batch, MLA_HEADS, nq),
        in_specs=[
            pl.BlockSpec((tq, MLA_QK), lambda b, h, i: (b * nq + i, h)),
            pl.BlockSpec((seq, MLA_NOPE), lambda b, h, i: (b, h)),
            pl.BlockSpec((seq, LANES), lambda b, h, i: (b, 0)),
            pl.BlockSpec((1, nq, MLA_VT_ROWS, tq), lambda b, h, i: (h, b, 0, 0)),
        ],
        out_specs=pl.BlockSpec((tq, MLA_V), lambda b, h, i: (b * nq + i, h)),
        out_shape=jax.ShapeDtypeStruct((T, MLA_HEADS * MLA_V), BF16),
        scratch_shapes=[pltpu.VMEM((1, tq), F32), pltpu.VMEM((MLA_VT_ROWS, tq), F32), pltpu.VMEM((tq, tq), F32)],
        compiler_params=_cparams("parallel", "parallel", "arbitrary"),
        name="mla_attn",
    )(q, kn, krd, vt)


def _merge_kernel(x_ref, ohg_ref, orw_ref, omla_ref, g0_ref, g1_ref, g2_ref, wb_ref, wo_ref, gp_ref, o_ref):
    merged = (_sigmoid(g0_ref[...]) * jnp.dot(ohg_ref[...], wb_ref[0], preferred_element_type=F32)
              + _sigmoid(g1_ref[...]) * jnp.dot(orw_ref[...], wb_ref[1], preferred_element_type=F32)
              + _sigmoid(g2_ref[...]) * jnp.dot(omla_ref[...], wb_ref[2], preferred_element_type=F32))
    y = _dot(merged, wo_ref[...])
    o_ref[...] = x_ref[...] + _rms(y, gp_ref[...])


def merge(x, o_hg, o_rw, o_mla, z_a, w_branch, w_out, g_post):
    T, D = x.shape
    tm = MERGE_ROWS
    row = pl.BlockSpec((tm, D), lambda i: (i, 0))

    def gate(n):
        return pl.BlockSpec((tm, D), lambda i, n=n: (i, ZA_GATE // D + n))

    def full(a):
        return pl.BlockSpec(a.shape, lambda i: (0,) * a.ndim)

    return pl.pallas_call(
        _merge_kernel,
        grid=(T // tm,),
        in_specs=[row, row, row, row, gate(0), gate(1), gate(2), full(w_branch), full(w_out), full(g_post)],
        out_specs=row,
        out_shape=jax.ShapeDtypeStruct((T, D), F32),
        compiler_params=_cparams("parallel"),
        name="merge",
    )(x, o_hg, o_rw, o_mla, z_a, z_a, z_a, w_branch, w_out, g_post)


FFN_TILE = 256


def _ffn_kernel(x_ref, gpre_ref, wg_ref, wu_ref, wo_ref, gpost_ref, o_ref, h_ref, acc_ref):
    j = pl.program_id(1)

    @pl.when(j == 0)
    def _():
        h_ref[...] = _rms(x_ref[...], gpre_ref[...]).astype(BF16)
        acc_ref[...] = jnp.zeros_like(acc_ref)

    h = h_ref[...]
    gate = jnp.dot(h, wg_ref[...], preferred_element_type=F32)
    up = jnp.dot(h, wu_ref[...], preferred_element_type=F32)
    act = gate * _sigmoid(gate) * up
    acc_ref[...] += _dot(act, wo_ref[...])

    @pl.when(j == pl.num_programs(1) - 1)
    def _():
        o_ref[...] = x_ref[...] + _rms(acc_ref[...], gpost_ref[...])


def ffn(x, g_pre, w_in, w_out, g_post):
    T, D = x.shape
    tm = FFN_ROWS
    tf = FFN_TILE
    nf = D_FF // tf
    return pl.pallas_call(
        _ffn_kernel,
        grid=(T // tm, nf),
        in_specs=[
            pl.BlockSpec((tm, D), lambda i, j: (i, 0)),
            pl.BlockSpec((1, D), lambda i, j: (0, 0)),
            pl.BlockSpec((D, tf), lambda i, j: (0, j)),
            pl.BlockSpec((D, tf), lambda i, j: (0, nf + j)),
            pl.BlockSpec((tf, D), lambda i, j: (j, 0)),
            pl.BlockSpec((1, D), lambda i, j: (0, 0)),
        ],
        out_specs=pl.BlockSpec((tm, D), lambda i, j: (i, 0)),
        out_shape=jax.ShapeDtypeStruct((T, D), F32),
        scratch_shapes=[pltpu.VMEM((tm, D), BF16), pltpu.VMEM((tm, D), F32)],
        compiler_params=_cparams("parallel", "arbitrary"),
        name="ffn",
    )(x, g_pre, w_in, w_in, w_out, g_post)


def _swap_halves(w):
    h = w.shape[-1] // 2
    return jnp.concatenate([w[..., h:], w[..., :h]], axis=-1)


def _pad_rows(w, rows, at):
    out = jnp.zeros((rows, w.shape[1]), w.dtype)
    return out.at[at:at + w.shape[0]].set(w)


def _pack_in_proj(w, w_vres):
    hg, rkv, wa, g, cq, ckv, kr, gates = jnp.split(
        w, np.cumsum([4096, 3072, RW_LORA_W + RW_LORA_A, RW_LORA_G, MLA_Q_RANK, MLA_KV_RANK, MLA_ROPE]).tolist(),
        axis=1)
    wide = jnp.concatenate([hg, rkv, gates], axis=1)
    parts = [ckv, kr, _swap_halves(kr), cq, wa, g]
    if w_vres is not None:
        parts.append(jnp.pad(w_vres, ((0, 0), (0, LANES - RW_LORA_V))))
    narrow = jnp.concatenate(parts, axis=1)
    return wide.astype(BF16), narrow.astype(BF16)


def _pack_wq(w_uq):
    w = w_uq.reshape(MLA_Q_RANK, MLA_HEADS, MLA_NOPE + MLA_ROPE)
    rope = w[..., MLA_NOPE:]
    w = jnp.concatenate([w[..., :MLA_NOPE], rope, _swap_halves(rope)], axis=-1)
    return w.reshape(MLA_Q_RANK, MLA_HEADS * MLA_QK).astype(BF16)


def _row(v):
    return v.reshape(1, -1).astype(F32)


def kernel(x, positions, hgrn_lb_logits, mix_pre_g, mix_post_g, ffn_pre_g, ffn_post_g, w_in, w_vres_down,
           hgrn_onorm_g, rwkv_mu, rwkv_vres_mu, rwkv_w0, rwkv_w_up, rwkv_a0, rwkv_a_up, rwkv_g_up, rwkv_v0,
           rwkv_v_up, rwkv_k_k, rwkv_k_a, rwkv_r_k, rwkv_lnx_g, rwkv_lnx_b, mla_q_norm_g, mla_w_uq,
           mla_kv_norm_g, mla_w_ukv, w_branch, w_out, w_ffn_in, w_ffn_out):
    batch, seq, d = x.shape
    depth = w_in.shape[0]
    T = batch * seq
    xt = x.reshape(T, d)

    probs = jax.nn.softmax(hgrn_lb_logits.astype(F32), axis=0)
    lower_bounds = jnp.cumsum(probs, axis=0) - probs[0]
    inv_freq = ROPE_THETA ** (-jnp.arange(0, MLA_ROPE, 2, dtype=F32) / MLA_ROPE)
    ang = positions.astype(F32).reshape(T, 1) * inv_freq
    cos, sin = jnp.cos(ang), jnp.sin(ang)
    cs = jnp.concatenate([cos, cos, -sin, sin], axis=-1)

    W = RW_WIDTH
    v_first = None
    for l in range(depth):
        w_wide, w_narrow = _pack_in_proj(w_in[l], None if l == 0 else w_vres_down[l - 1])
        g_pre = _row(mix_pre_g[l])
        z_a = norm_matmul(xt, g_pre, w_wide, PROJ_COLS)
        z_b = norm_matmul(xt, g_pre, w_narrow, w_narrow.shape[1])

        o_hg = hgrn(z_a, _row(lower_bounds[l]), _row(hgrn_onorm_g[l]), batch, seq)

        mu = rwkv_mu[l]
        p = {
            "mu_r": _row(mu[:W]), "mu_k": _row(mu[W:2 * W]), "mu_v": _row(mu[2 * W:3 * W]),
            "mu_wa": _row(mu[3 * W:3 * W + LANES]), "mu_g": _row(mu[3 * W + LANES:]),
            "w0": _row(rwkv_w0[l]), "w_up": _pad_rows(rwkv_w_up[l], LANES, 0).astype(BF16),
            "a0": _row(rwkv_a0[l]), "a_up": _pad_rows(rwkv_a_up[l], LANES, RW_LORA_W).astype(BF16),
            "g_up": rwkv_g_up[l].astype(BF16),
            "k_k": _row(rwkv_k_k[l]), "k_a": _row(rwkv_k_a[l]),
        }
        if l > 0:
            p["mu_zv"] = _row(jnp.pad(rwkv_vres_mu[l - 1], (0, LANES - RW_LORA_V)))
            p["v0"] = _row(rwkv_v0[l - 1])
            p["v_up"] = _pad_rows(rwkv_v_up[l - 1], LANES, 0).astype(BF16)
        r, ld, k, v, kk, a, g = rw_prep(z_a, z_b, v_first, p, seq)
        if l == 0:
            v_first = v
        o_rw = rw_chunk(r, ld, k, v, kk, a, g, _row(rwkv_r_k[l]), _row(rwkv_lnx_g[l]), _row(rwkv_lnx_b[l]),
                        batch, seq)

        q, kn, vt, krd = mla_up(z_b, cs, _row(mla_q_norm_g[l]), _row(mla_kv_norm_g[l]), _pack_wq(mla_w_uq[l]),
                                mla_w_ukv[l].astype(BF16))
        o_mla = mla_attn(q, kn, vt, krd, batch, seq)

        xt = merge(xt, o_hg, o_rw, o_mla, z_a, w_branch[l].astype(BF16), w_out[l].astype(BF16),
                   _row(mix_post_g[l]))
        xt = ffn(xt, _row(ffn_pre_g[l]), w_ffn_in[l].astype(BF16), w_ffn_out[l].astype(BF16),
                 _row(ffn_post_g[l]))
    return xt.reshape(batch, seq, d)
```

```python
import functools
import math

import jax
import jax.numpy as jnp
import numpy as np
from jax import lax
from jax.experimental import pallas as pl
from jax.experimental.pallas import tpu as pltpu

F32 = jnp.float32
BF16 = jnp.bfloat16
LOG2E = math.log2(math.e)

D_MODEL = 1024
NORM_EPS = 1e-6
MASK_VALUE = -1e30
MIN_GATE = 1e-30
HG_HEADS = 8
HG_DK = 128
RW_HEADS = 16
RW_N = 64
RW_WIDTH = RW_HEADS * RW_N
RW_LORA_W = 64
RW_LORA_A = 64
RW_LORA_G = 128
RW_LORA_V = 32
RW_LNX_EPS = 1e-5 * RW_N
MLA_HEADS = 8
MLA_Q_RANK = 384
MLA_KV_RANK = 256
MLA_NOPE = 128
MLA_ROPE = 64
MLA_V = 128
ROPE_THETA = 10000.0
D_FF = 2816

LANES = 128
SUBLANES = 8
VMEM_LIMIT = 48 * 1024 * 1024

PROJ_ROWS = 1024
PROJ_COLS = 3072
RW_PREP_ROWS = 256
MLA_UP_ROWS = 512
MERGE_ROWS = 256
FFN_ROWS = 1024

ZA_HG, ZA_RKV, ZA_GATE, ZA_WIDTH = 0, 3072, 6144, 9216
ZB_F, ZB_CKV, ZB_KR, ZB_WA, ZB_CQ, ZB_G, ZB_VRES = 0, 1024, 1280, 1408, 1536, 1920, 2048


def _cparams(*sem):
    return pltpu.CompilerParams(dimension_semantics=sem, vmem_limit_bytes=VMEM_LIMIT)


def _sigmoid(x):
    return 1.0 / (1.0 + jnp.exp(-x))


def _rms(x, g):
    ms = jnp.mean(x * x, axis=-1, keepdims=True)
    return x * lax.rsqrt(ms + NORM_EPS) * g


def _dot(a, b):
    return jnp.dot(a.astype(BF16), b.astype(BF16), preferred_element_type=F32)


def _dot_nt(a, b):
    return lax.dot_general(a.astype(BF16), b.astype(BF16), (((1,), (1,)), ((), ())), preferred_element_type=F32)


def _dot_tn(a, b):
    return lax.dot_general(a.astype(BF16), b.astype(BF16), (((0,), (0,)), ((), ())), preferred_element_type=F32)


def _tril_ones(n):
    row = lax.broadcasted_iota(jnp.int32, (n, n), 0)
    col = lax.broadcasted_iota(jnp.int32, (n, n), 1)
    return jnp.where(col <= row, 1.0, 0.0).astype(BF16)


def _cumsum_rows(tril, x):
    n = x.shape[1]
    hi = x.astype(BF16)
    r1 = x - hi.astype(F32)
    mid = r1.astype(BF16)
    lo = (r1 - mid.astype(F32)).astype(BF16)
    y = jnp.dot(tril, jnp.concatenate([hi, mid, lo], axis=1), preferred_element_type=F32)
    return y[:, :n] + y[:, n:2 * n] + y[:, 2 * n:]


def _norm_matmul_kernel(x_ref, g_ref, w_ref, o_ref, h_ref):
    @pl.when(pl.program_id(1) == 0)
    def _():
        h_ref[...] = _rms(x_ref[...], g_ref[...]).astype(BF16)

    o_ref[...] = jnp.dot(h_ref[...], w_ref[...], preferred_element_type=F32).astype(o_ref.dtype)


def norm_matmul(x, g, w, tn, out_dtype):
    T, D = x.shape
    N = w.shape[1]
    tm = PROJ_ROWS
    return pl.pallas_call(
        _norm_matmul_kernel,
        grid=(T // tm, N // tn),
        in_specs=[
            pl.BlockSpec((tm, D), lambda i, j: (i, 0)),
            pl.BlockSpec((1, D), lambda i, j: (0, 0)),
            pl.BlockSpec((D, tn), lambda i, j: (0, j)),
        ],
        out_specs=pl.BlockSpec((tm, tn), lambda i, j: (i, j)),
        out_shape=jax.ShapeDtypeStruct((T, N), out_dtype),
        scratch_shapes=[pltpu.VMEM((tm, D), BF16)],
        compiler_params=_cparams("parallel", "arbitrary"),
        name="norm_matmul",
    )(x, g, w)


HG_CHUNK = 128


def _hgrn_term_codes(C):
    t = np.arange(C)[:, None]
    s = np.arange(C)[None, :]
    codes = np.where(s == t, 0, -1)
    m, i = 1, 0
    while m < C:
        hit = ((t // m) % 2 == 1) & (s // m == t // m - 1)
        codes = np.where(hit, 1 + i, codes)
        m, i = 2 * m, i + 1
    return codes.astype(np.int32)


def _hgrn_kernel(q_ref, f_ref, v_ref, og_ref, lb_ref, gn_ref, code_ref, o_ref, st_ref):
    C = q_ref.shape[0]
    heads = range(q_ref.shape[1] // HG_DK)

    @pl.when(pl.program_id(1) == 0)
    def _():
        st_ref[...] = jnp.zeros_like(st_ref)

    def hs(ref, h):
        return ref[:, h * HG_DK:(h + 1) * HG_DK].astype(F32)

    code = code_ref[...]
    tril = _tril_ones(C)
    ones = jnp.ones((HG_DK, C), BF16)
    g8 = C // SUBLANES
    rowc = lax.broadcasted_iota(jnp.int32, (C, HG_DK), 0)
    sub = rowc & (SUBLANES - 1)

    def boundary(bh, m):
        if m >= SUBLANES:
            pieces = [jnp.broadcast_to(bh[r0:r0 + 1, :], (2 * m, HG_DK)) for r0 in range(m - 1, C, 2 * m)]
            return pieces[0] if len(pieces) == 1 else jnp.concatenate(pieces, axis=0)
        if m == 1:
            return jnp.where((sub & 1) == 0, bh, pltpu.roll(bh, shift=1, axis=0))
        b3 = bh.reshape(g8, SUBLANES, HG_DK)

        def tile_row(r):
            return jnp.broadcast_to(b3[:, r:r + 1, :], (g8, SUBLANES, HG_DK)).reshape(C, HG_DK)

        if m == 2:
            return jnp.where(sub < 4, tile_row(1), tile_row(5))
        return tile_row(3)

    q = [hs(q_ref, h) for h in heads]
    zf = [hs(f_ref, h) for h in heads]
    v = [hs(v_ref, h) for h in heads]
    lb = [hs(lb_ref, h) for h in heads]
    lf = [jnp.log2(jnp.maximum(lb[h] + (1.0 - lb[h]) * _sigmoid(zf[h]), MIN_GATE)) for h in heads]
    k = [(1.0 - lb[h]) * _sigmoid(-zf[h]) for h in heads]
    b = [_cumsum_rows(tril, lf[h]) for h in heads]

    c0 = [jnp.dot((q[h] * k[h]).astype(BF16), ones, preferred_element_type=F32) for h in heads]
    scores = [jnp.where(code == 0, c0[h], 0.0) for h in heads]
    m, i = 1, 0
    while m < C:
        odd = ((rowc >> i) & 1) == 1
        sign = jnp.where(odd, 1.0, -1.0)
        hit = code == 1 + i
        xm = []
        for h in heads:
            d = (b[h] - boundary(b[h], m)) * sign
            xm.append((jnp.where(odd, q[h], k[h]) * jnp.exp2(d)).astype(BF16))
        gm = [_dot_nt(xm[h], xm[h]) for h in heads]
        scores = [jnp.where(hit, gm[h], scores[h]) for h in heads]
        m, i = 2 * m, i + 1

    st = [st_ref[h] for h in heads]
    o = [_dot(scores[h], v[h]) + _dot_nt(q[h] * jnp.exp2(b[h]), st[h]) for h in heads]
    bl = [b[h][C - 1:C, :] for h in heads]
    for h in heads:
        st_ref[h] = st[h] * jnp.exp2(bl[h]) + _dot_tn(v[h], k[h] * jnp.exp2(bl[h] - b[h]))
    for h in heads:
        og = hs(og_ref, h)
        o_ref[:, h * HG_DK:(h + 1) * HG_DK] = (_rms(o[h], gn_ref[...]) * (og * _sigmoid(og))).astype(o_ref.dtype)


def hgrn(z_a, z_b, lower_bound, onorm_g, batch, seq):
    T = z_a.shape[0]
    C = HG_CHUNK
    nc = seq // C
    W = HG_HEADS * HG_DK
    codes = jnp.asarray(_hgrn_term_codes(C))

    def zspec(seg):
        return pl.BlockSpec((C, W), lambda b, c, seg=seg: (b * nc + c, ZA_HG // W + seg))

    return pl.pallas_call(
        _hgrn_kernel,
        grid=(batch, nc),
        in_specs=[
            zspec(0), pl.BlockSpec((C, W), lambda b, c: (b * nc + c, ZB_F // W)), zspec(1), zspec(2),
            pl.BlockSpec((1, W), lambda b, c: (0, 0)),
            pl.BlockSpec((1, HG_DK), lambda b, c: (0, 0)),
            pl.BlockSpec((C, C), lambda b, c: (0, 0)),
        ],
        out_specs=pl.BlockSpec((C, W), lambda b, c: (b * nc + c, 0)),
        out_shape=jax.ShapeDtypeStruct((T, W), BF16),
        scratch_shapes=[pltpu.VMEM((HG_HEADS, HG_DK, HG_DK), F32)],
        compiler_params=_cparams("parallel", "arbitrary"),
        name="hgrn",
    )(z_a, z_b, z_a, z_a, lower_bound, onorm_g, codes)


def _head_seg():
    r = lax.broadcasted_iota(jnp.int32, (LANES, LANES), 0)
    c = lax.broadcasted_iota(jnp.int32, (LANES, LANES), 1)
    lg = RW_N.bit_length() - 1
    seg = jnp.where((r >> lg) == (c >> lg), 1.0, 0.0).astype(BF16)
    return jnp.concatenate([seg, seg], axis=0)


def _head_sum(x, seg):
    hi = x.astype(BF16)
    lo = (x - hi.astype(F32)).astype(BF16)
    return jnp.dot(jnp.concatenate([hi, lo], axis=1), seg, preferred_element_type=F32)


PREV_ROWS = 16


def _shift_mix(cur_ref, prev_ref, mu, first):
    z = cur_ref[...].astype(F32)
    last = prev_ref[PREV_ROWS - 1:PREV_ROWS, :].astype(F32)
    last = jnp.where(first, jnp.zeros_like(last), last)
    rolled = pltpu.roll(z, shift=1, axis=0)
    rid = lax.broadcasted_iota(jnp.int32, z.shape, 0)
    zp = jnp.where(rid == 0, jnp.broadcast_to(last, z.shape), rolled)
    return z + (zp - z) * mu


def _rw_prep_kernel(has_vres, seq, *refs):
    if has_vres:
        (r_ref, rp_ref, k_ref, kp_ref, v_ref, vp_ref, wa_ref, wap_ref, g_ref, gp_ref, zv_ref, zvp_ref, vf_ref,
         mu_r, mu_k, mu_v, mu_wa, mu_g, mu_zv, w0, w_up, a0, a_up, g_up, v0, v_up, k_k, k_a,
         ro, ldo, ko, vo, kko, ao, go) = refs
    else:
        (r_ref, rp_ref, k_ref, kp_ref, v_ref, vp_ref, wa_ref, wap_ref, g_ref, gp_ref,
         mu_r, mu_k, mu_v, mu_wa, mu_g, w0, w_up, a0, a_up, g_up, k_k, k_a,
         ro, ldo, ko, vo, kko, ao, go) = refs
    tm = r_ref.shape[0]
    first = (pl.program_id(0) * tm) % seq == 0

    r = _shift_mix(r_ref, rp_ref, mu_r[...], first)
    k = _shift_mix(k_ref, kp_ref, mu_k[...], first)
    v = _shift_mix(v_ref, vp_ref, mu_v[...], first)
    wa = _shift_mix(wa_ref, wap_ref, mu_wa[...], first)
    zg = _shift_mix(g_ref, gp_ref, mu_g[...], first)

    zw = w0[...] + _dot(jnp.tanh(wa), w_up[...])
    w_log = -(jnp.maximum(-zw, 0.0) + jnp.log(1.0 + jnp.exp(-jnp.abs(zw)))) - 0.5
    ldo[...] = -jnp.exp(w_log)
    a = _sigmoid(a0[...] + _dot(wa, a_up[...]))
    go[...] = _dot(_sigmoid(zg), g_up[...]).astype(go.dtype)
    if has_vres:
        zv = _shift_mix(zv_ref, zvp_ref, mu_zv[...], first)
        v = v + (vf_ref[...].astype(F32) - v) * _sigmoid(v0[...] + _dot(zv, v_up[...]))
    kk = k * k_k[...]
    seg = _head_seg()
    for p in range(RW_WIDTH // LANES):
        sl = slice(p * LANES, (p + 1) * LANES)
        kkp = kk[:, sl]
        ss = _head_sum(kkp * kkp, seg)
        kko[:, sl] = (kkp * lax.rsqrt(jnp.maximum(ss, 1e-24))).astype(kko.dtype)
    ro[...] = r.astype(ro.dtype)
    ko[...] = (k * (1.0 + (a - 1.0) * k_a[...])).astype(ko.dtype)
    vo[...] = v.astype(vo.dtype)
    ao[...] = a.astype(ao.dtype)


def rw_prep(z_a, z_b, v_first, p, seq):
    T = z_a.shape[0]
    has_vres = v_first is not None
    W = RW_WIDTH
    tm = RW_PREP_ROWS
    tb = tm // PREV_ROWS

    def cur(width, off):
        return pl.BlockSpec((tm, width), lambda i: (i, off // width))

    def prev(width, off):
        return pl.BlockSpec((PREV_ROWS, width), lambda i: (jnp.maximum(i * tb - 1, 0), off // width))

    def full(a):
        return pl.BlockSpec(a.shape, lambda i: (0,) * a.ndim)

    acts = [z_a, z_a, z_a, z_a, z_a, z_a, z_b, z_b, z_b, z_b]
    specs = [cur(W, ZA_RKV), prev(W, ZA_RKV), cur(W, ZA_RKV + W), prev(W, ZA_RKV + W),
             cur(W, ZA_RKV + 2 * W), prev(W, ZA_RKV + 2 * W),
             cur(LANES, ZB_WA), prev(LANES, ZB_WA), cur(LANES, ZB_G), prev(LANES, ZB_G)]
    if has_vres:
        acts += [z_b, z_b, v_first]
        specs += [cur(LANES, ZB_VRES), prev(LANES, ZB_VRES), pl.BlockSpec((tm, W), lambda i: (i, 0))]
        names = ["mu_r", "mu_k", "mu_v", "mu_wa", "mu_g", "mu_zv", "w0", "w_up", "a0", "a_up", "g_up", "v0", "v_up",
                 "k_k", "k_a"]
    else:
        names = ["mu_r", "mu_k", "mu_v", "mu_wa", "mu_g", "w0", "w_up", "a0", "a_up", "g_up", "k_k", "k_a"]
    params = [p[n] for n in names]
    out_spec = pl.BlockSpec((tm, W), lambda i: (i, 0))
    return pl.pallas_call(
        functools.partial(_rw_prep_kernel, has_vres, seq),
        grid=(T // tm,),
        in_specs=specs + [full(a) for a in params],
        out_specs=[out_spec] * 7,
        out_shape=[jax.ShapeDtypeStruct((T, W), F32 if i == 1 else BF16) for i in range(7)],
        compiler_params=_cparams("parallel"),
        name="rw_prep",
    )(*acts, *params)


RW_CHUNK = 64


def _rw_chunk_kernel(r_ref, ld_ref, k_ref, v_ref, kk_ref, a_ref, g_ref, rk_ref, lg_ref, lb_ref, o_ref, st_ref):
    C = r_ref.shape[0]
    C2 = 2 * C
    pairs = range(r_ref.shape[1] // LANES)

    @pl.when(pl.program_id(1) == 0)
    def _():
        st_ref[...] = jnp.zeros_like(st_ref)

    def ps(ref, p):
        return ref[:, p * LANES:(p + 1) * LANES].astype(F32)

    tril = _tril_ones(C)
    seg = _head_seg()
    lane = lax.broadcasted_iota(jnp.int32, (C, LANES), 1)
    row = lax.broadcasted_iota(jnp.int32, (C2, C2), 0)
    col = lax.broadcasted_iota(jnp.int32, (C2, C2), 1)
    strict = (col & (C - 1)) < (row & (C - 1))
    incl = (col & (C - 1)) <= (row & (C - 1))
    eye = jnp.where(row == col, 1.0, 0.0)

    def stack(x):
        return jnp.concatenate([jnp.where(lane < RW_N, x, 0.0), jnp.where(lane >= RW_N, x, 0.0)], axis=0).astype(BF16)

    r = [ps(r_ref, p) for p in pairs]
    ld = [ps(ld_ref, p) for p in pairs]
    k = [ps(k_ref, p) for p in pairs]
    v = [ps(v_ref, p) for p in pairs]
    kk = [ps(kk_ref, p) for p in pairs]
    kka = [kk[p] * ps(a_ref, p) for p in pairs]
    cl_all = _cumsum_rows(tril, ld_ref[...])
    cl = [cl_all[:, p * LANES:(p + 1) * LANES] for p in pairs]
    gc = [cl[p][C - 1:C, :] for p in pairs]
    e_inv = [jnp.exp(-cl[p]) for p in pairs]
    e_end = [jnp.exp(gc[p] - cl[p]) for p in pairs]
    ar = [jnp.concatenate([stack(-kk[p] * jnp.exp(cl[p] - ld[p])), stack(r[p] * jnp.exp(cl[p]))], axis=0)
          for p in pairs]
    bk = [jnp.concatenate([stack(kka[p] * e_inv[p]), stack(k[p] * e_inv[p])], axis=0) for p in pairs]
    bkg = [jnp.concatenate([stack(kka[p] * e_end[p]), stack(k[p] * e_end[p])], axis=0) for p in pairs]
    vv = [stack(v[p]) for p in pairs]

    g4 = [_dot_nt(ar[p], bk[p]) for p in pairs]
    lab = [jnp.where(strict, g4[p][:C2, :C2], 0.0) for p in pairs]
    lkm = [jnp.concatenate([jnp.where(strict, g4[p][:C2, C2:], 0.0), jnp.where(incl, g4[p][C2:, C2:], 0.0)],
                           axis=0).astype(BF16) for p in pairs]
    mb = [jnp.where(incl, g4[p][C2:, :C2], 0.0).astype(BF16) for p in pairs]
    qi = [eye + lab[p] for p in pairs]
    mp = [_dot(lab[p], lab[p]) for p in pairs]
    n = 2
    while 2 * n < C:
        mq = [_dot(mp[p], jnp.concatenate([mp[p].astype(BF16), qi[p].astype(BF16)], axis=1)) for p in pairs]
        mp = [mq[p][:, :C2] for p in pairs]
        qi = [qi[p] + mq[p][:, C2:] for p in pairs]
        n *= 2
    qi = [qi[p] + _dot(mp[p], qi[p]) for p in pairs]

    zt = [st_ref[p] for p in pairs]
    fz = [_dot(jnp.concatenate([ar[p], lkm[p]], axis=1), jnp.concatenate([zt[p].T.astype(BF16), vv[p]], axis=0))
          for p in pairs]
    uu = [_dot(qi[p], fz[p][:C2]) for p in pairs]
    yy = [fz[p][C2:] + _dot(mb[p], uu[p]) for p in pairs]
    for p in pairs:
        uv = jnp.concatenate([uu[p].astype(BF16), vv[p]], axis=0)
        st_ref[p] = zt[p] * jnp.exp(gc[p]) + _dot_tn(uv, bkg[p])

    y = [yy[p][:C] + yy[p][C:] for p in pairs]
    mean = [_head_sum(y[p], seg) * (1.0 / RW_N) for p in pairs]
    d = [y[p] - mean[p] for p in pairs]
    var = [_head_sum(d[p] * d[p], seg) * (1.0 / RW_N) for p in pairs]
    bonus = [_head_sum(r[p] * k[p] * ps(rk_ref, p), seg) * v[p] for p in pairs]
    for p in pairs:
        yn = d[p] * lax.rsqrt(var[p] + RW_LNX_EPS) * ps(lg_ref, p) + ps(lb_ref, p)
        o_ref[:, p * LANES:(p + 1) * LANES] = ((yn + bonus[p]) * ps(g_ref, p)).astype(o_ref.dtype)


def rw_chunk(r, ld, k, v, kk, a, g, r_k, lnx_g, lnx_b, batch, seq):
    T, W = r.shape
    C = RW_CHUNK
    nc = seq // C
    act = pl.BlockSpec((C, W), lambda b, c: (b * nc + c, 0))
    par = pl.BlockSpec((1, W), lambda b, c: (0, 0))
    return pl.pallas_call(
        _rw_chunk_kernel,
        grid=(batch, nc),
        in_specs=[act] * 7 + [par] * 3,
        out_specs=act,
        out_shape=jax.ShapeDtypeStruct((T, W), BF16),
        scratch_shapes=[pltpu.VMEM((W // LANES, LANES, LANES), F32)],
        compiler_params=_cparams("parallel", "arbitrary"),
        name="rw_chunk",
    )(r, ld, k, v, kk, a, g, r_k, lnx_g, lnx_b)


MLA_QK = 2 * LANES
MLA_TQ = 512
MLA_HEADS_PER_STEP = 4
MLA_VT_ROWS = MLA_V + 16


def _mla_up_kernel(cq_ref, ckv_ref, kr_ref, cs_ref, gq_ref, gkv_ref, wq_ref, wkv_ref, q_ref, kn_ref, vt_ref, krd_ref):
    scale = (MLA_NOPE + MLA_ROPE) ** -0.5 * LOG2E
    cs = cs_ref[...]
    q = _dot(_rms(cq_ref[...], gq_ref[...]), wq_ref[...])
    for h in range(MLA_HEADS):
        o = h * MLA_QK
        q_ref[:, o:o + LANES] = (q[:, o:o + LANES] * scale).astype(q_ref.dtype)
        q_ref[:, o + LANES:o + 2 * LANES] = (q[:, o + LANES:o + 2 * LANES] * (cs * scale)).astype(q_ref.dtype)
    kv = _dot(_rms(ckv_ref[...], gkv_ref[...]), wkv_ref[...])
    for h in range(MLA_HEADS):
        o = h * (MLA_NOPE + MLA_V)
        kn_ref[:, h * MLA_NOPE:(h + 1) * MLA_NOPE] = kv[:, o:o + MLA_NOPE].astype(kn_ref.dtype)
        vt_ref[h, 0, :MLA_V, :] = kv[:, o + MLA_NOPE:o + MLA_NOPE + MLA_V].T.astype(vt_ref.dtype)
        vt_ref[h, 0, MLA_V:, :] = jnp.ones((MLA_VT_ROWS - MLA_V, kv.shape[0]), vt_ref.dtype)
    krx = kr_ref[...] * cs
    krd_ref[...] = (krx + pltpu.roll(krx, shift=MLA_ROPE, axis=1)).astype(krd_ref.dtype)


def mla_up(z_b, cs, gq, gkv, wq, wkv):
    T = z_b.shape[0]
    tm = MLA_TQ
    nq = wq.shape[1]

    def full(a):
        return pl.BlockSpec(a.shape, lambda i: (0,) * a.ndim)

    return pl.pallas_call(
        _mla_up_kernel,
        grid=(T // tm,),
        in_specs=[
            pl.BlockSpec((tm, MLA_Q_RANK), lambda i: (i, ZB_CQ // MLA_Q_RANK)),
            pl.BlockSpec((tm, MLA_KV_RANK), lambda i: (i, ZB_CKV // MLA_KV_RANK)),
            pl.BlockSpec((tm, LANES), lambda i: (i, ZB_KR // LANES)),
            pl.BlockSpec((tm, LANES), lambda i: (i, 0)),
            full(gq), full(gkv), full(wq), full(wkv),
        ],
        out_specs=[
            pl.BlockSpec((tm, nq), lambda i: (i, 0)),
            pl.BlockSpec((tm, MLA_HEADS * MLA_NOPE), lambda i: (i, 0)),
            pl.BlockSpec((MLA_HEADS, 1, MLA_VT_ROWS, tm), lambda i: (0, i, 0, 0)),
            pl.BlockSpec((tm, LANES), lambda i: (i, 0)),
        ],
        out_shape=[
            jax.ShapeDtypeStruct((T, nq), BF16),
            jax.ShapeDtypeStruct((T, MLA_HEADS * MLA_NOPE), BF16),
            jax.ShapeDtypeStruct((MLA_HEADS, T // tm, MLA_VT_ROWS, tm), BF16),
            jax.ShapeDtypeStruct((T, LANES), BF16),
        ],
        compiler_params=_cparams("parallel"),
        name="mla_up",
    )(z_b, z_b, z_b, cs, gq, gkv, wq, wkv)


def _mla_attn_kernel(q_ref, kn_ref, kr_ref, vt_ref, o_ref, m_ref, acc_ref, s_ref):
    tq = q_ref.shape[0]
    heads = range(q_ref.shape[1] // MLA_QK)
    qi = pl.program_id(2)
    q = [q_ref[:, h * MLA_QK:(h + 1) * MLA_QK] for h in heads]
    m_ref[...] = jnp.full_like(m_ref, -jnp.inf)
    acc_ref[...] = jnp.zeros_like(acc_ref)
    row = lax.broadcasted_iota(jnp.int32, (tq, tq), 0)
    col = lax.broadcasted_iota(jnp.int32, (tq, tq), 1)
    causal = row <= col

    def scores(kb):
        ks = pl.multiple_of(kb * tq, tq)
        kr = kr_ref[pl.ds(ks, tq), :]
        kc = [jnp.concatenate([kn_ref[pl.ds(ks, tq), h * MLA_NOPE:(h + 1) * MLA_NOPE], kr], axis=1) for h in heads]
        return [lax.dot_general(kc[h], q[h], (((1,), (1,)), ((), ())), preferred_element_type=F32)
                for h in heads]

    def update(kb):
        s = [s_ref[h] for h in heads]
        m_old = [m_ref[h] for h in heads]
        m_new = [jnp.maximum(m_old[h], jnp.max(s[h], axis=0, keepdims=True)) for h in heads]
        p = [jnp.exp2(s[h] - m_new[h]).astype(BF16) for h in heads]
        for h in heads:
            acc_ref[h] = (jnp.exp2(m_old[h] - m_new[h]) * acc_ref[h]
                          + jnp.dot(vt_ref[h, kb], p[h], preferred_element_type=F32))
            m_ref[h] = m_new[h]

    s0 = scores(0)
    for h in heads:
        s_ref[h] = jnp.where(qi > 0, s0[h], jnp.where(causal, s0[h], MASK_VALUE))

    def body(kb, carry):
        s_next = scores(kb + 1)
        update(kb)
        for h in heads:
            s_ref[h] = s_next[h]
        return carry

    lax.fori_loop(0, qi - 1, body, 0)

    @pl.when(qi > 0)
    def _():
        s_next = scores(qi)
        update(qi - 1)
        for h in heads:
            s_ref[h] = jnp.where(causal, s_next[h], MASK_VALUE)

    update(qi)
    for h in heads:
        acc = acc_ref[h]
        o_ref[:, h * MLA_V:(h + 1) * MLA_V] = (acc[:MLA_V] / acc[MLA_V:MLA_V + 1]).T.astype(o_ref.dtype)


def mla_attn(q, kn, vt, krd, batch, seq):
    T = q.shape[0]
    tq = MLA_TQ
    nq = seq // tq
    g = MLA_HEADS_PER_STEP
    return pl.pallas_call(
        _mla_attn_kernel,
        grid=(batch, MLA_HEADS // g, nq),
        in_specs=[
            pl.BlockSpec((tq, g * MLA_QK), lambda b, h, i: (b * nq + i, h)),
            pl.BlockSpec((seq, g * MLA_NOPE), lambda b, h, i: (b, h)),
            pl.BlockSpec((seq, LANES), lambda b, h, i: (b, 0)),
            pl.BlockSpec((g, nq, MLA_VT_ROWS, tq), lambda b, h, i: (h, b, 0, 0)),
        ],
        out_specs=pl.BlockSpec((tq, g * MLA_V), lambda b, h, i: (b * nq + i, h)),
        out_shape=jax.ShapeDtypeStruct((T, MLA_HEADS * MLA_V), BF16),
        scratch_shapes=[pltpu.VMEM((g, 1, tq), F32), pltpu.VMEM((g, MLA_VT_ROWS, tq), F32),
                        pltpu.VMEM((g, tq, tq), F32)],
        compiler_params=_cparams("parallel", "parallel", "arbitrary"),
        name="mla_attn",
    )(q, kn, krd, vt)


def _merge_kernel(x_ref, ohg_ref, orw_ref, omla_ref, g0_ref, g1_ref, g2_ref, wb_ref, wo_ref, gp_ref, o_ref):
    merged = (_sigmoid(g0_ref[...].astype(F32)) * jnp.dot(ohg_ref[...], wb_ref[0], preferred_element_type=F32)
              + _sigmoid(g1_ref[...].astype(F32)) * jnp.dot(orw_ref[...], wb_ref[1], preferred_element_type=F32)
              + _sigmoid(g2_ref[...].astype(F32)) * jnp.dot(omla_ref[...], wb_ref[2], preferred_element_type=F32))
    y = _dot(merged, wo_ref[...])
    o_ref[...] = x_ref[...] + _rms(y, gp_ref[...])


def merge(x, o_hg, o_rw, o_mla, z_a, w_branch, w_out, g_post):
    T, D = x.shape
    tm = MERGE_ROWS
    row = pl.BlockSpec((tm, D), lambda i: (i, 0))

    def gate(n):
        return pl.BlockSpec((tm, D), lambda i, n=n: (i, ZA_GATE // D + n))

    def full(a):
        return pl.BlockSpec(a.shape, lambda i: (0,) * a.ndim)

    return pl.pallas_call(
        _merge_kernel,
        grid=(T // tm,),
        in_specs=[row, row, row, row, gate(0), gate(1), gate(2), full(w_branch), full(w_out), full(g_post)],
        out_specs=row,
        out_shape=jax.ShapeDtypeStruct((T, D), F32),
        compiler_params=_cparams("parallel"),
        name="merge",
    )(x, o_hg, o_rw, o_mla, z_a, z_a, z_a, w_branch, w_out, g_post)


FFN_TILE = 256


def _ffn_kernel(x_ref, gpre_ref, wg_ref, wu_ref, wo_ref, gpost_ref, o_ref, h_ref, acc_ref):
    j = pl.program_id(1)

    @pl.when(j == 0)
    def _():
        h_ref[...] = _rms(x_ref[...], gpre_ref[...]).astype(BF16)
        acc_ref[...] = jnp.zeros_like(acc_ref)

    h = h_ref[...]
    gate = jnp.dot(h, wg_ref[...], preferred_element_type=F32)
    up = jnp.dot(h, wu_ref[...], preferred_element_type=F32)
    act = gate * _sigmoid(gate) * up
    acc_ref[...] += _dot(act, wo_ref[...])

    @pl.when(j == pl.num_programs(1) - 1)
    def _():
        o_ref[...] = x_ref[...] + _rms(acc_ref[...], gpost_ref[...])


def ffn(x, g_pre, w_in, w_out, g_post):
    T, D = x.shape
    tm = FFN_ROWS
    tf = FFN_TILE
    nf = D_FF // tf
    return pl.pallas_call(
        _ffn_kernel,
        grid=(T // tm, nf),
        in_specs=[
            pl.BlockSpec((tm, D), lambda i, j: (i, 0)),
            pl.BlockSpec((1, D), lambda i, j: (0, 0)),
            pl.BlockSpec((D, tf), lambda i, j: (0, j)),
            pl.BlockSpec((D, tf), lambda i, j: (0, nf + j)),
            pl.BlockSpec((tf, D), lambda i, j: (j, 0)),
            pl.BlockSpec((1, D), lambda i, j: (0, 0)),
        ],
        out_specs=pl.BlockSpec((tm, D), lambda i, j: (i, 0)),
        out_shape=jax.ShapeDtypeStruct((T, D), F32),
        scratch_shapes=[pltpu.VMEM((tm, D), BF16), pltpu.VMEM((tm, D), F32)],
        compiler_params=_cparams("parallel", "arbitrary"),
        name="ffn",
    )(x, g_pre, w_in, w_in, w_out, g_post)


def _swap_halves(w):
    h = w.shape[-1] // 2
    return jnp.concatenate([w[..., h:], w[..., :h]], axis=-1)


def _pad_rows(w, rows, at):
    out = jnp.zeros((rows, w.shape[1]), w.dtype)
    return out.at[at:at + w.shape[0]].set(w)


def _pack_in_proj(w, w_vres):
    hg, rkv, wa, g, cq, ckv, kr, gates = jnp.split(
        w, np.cumsum([4096, 3072, RW_LORA_W + RW_LORA_A, RW_LORA_G, MLA_Q_RANK, MLA_KV_RANK, MLA_ROPE]).tolist(),
        axis=1)
    hq, hf, hi, hog = jnp.split(hg, 4, axis=1)
    wide = jnp.concatenate([hq, hi, hog, rkv, gates], axis=1)
    parts = [hf, ckv, kr, _swap_halves(kr), wa, cq, g]
    if w_vres is not None:
        parts.append(jnp.pad(w_vres, ((0, 0), (0, LANES - RW_LORA_V))))
    narrow = jnp.concatenate(parts, axis=1)
    return wide.astype(BF16), narrow.astype(BF16)


def _pack_wq(w_uq):
    w = w_uq.reshape(MLA_Q_RANK, MLA_HEADS, MLA_NOPE + MLA_ROPE)
    rope = w[..., MLA_NOPE:]
    w = jnp.concatenate([w[..., :MLA_NOPE], rope, _swap_halves(rope)], axis=-1)
    return w.reshape(MLA_Q_RANK, MLA_HEADS * MLA_QK).astype(BF16)


def _row(v):
    return v.reshape(1, -1).astype(F32)


def kernel(x, positions, hgrn_lb_logits, mix_pre_g, mix_post_g, ffn_pre_g, ffn_post_g, w_in, w_vres_down,
           hgrn_onorm_g, rwkv_mu, rwkv_vres_mu, rwkv_w0, rwkv_w_up, rwkv_a0, rwkv_a_up, rwkv_g_up, rwkv_v0,
           rwkv_v_up, rwkv_k_k, rwkv_k_a, rwkv_r_k, rwkv_lnx_g, rwkv_lnx_b, mla_q_norm_g, mla_w_uq,
           mla_kv_norm_g, mla_w_ukv, w_branch, w_out, w_ffn_in, w_ffn_out):
    batch, seq, d = x.shape
    depth = w_in.shape[0]
    T = batch * seq
    xt = x.reshape(T, d)

    probs = jax.nn.softmax(hgrn_lb_logits.astype(F32), axis=0)
    lower_bounds = jnp.cumsum(probs, axis=0) - probs[0]
    inv_freq = ROPE_THETA ** (-jnp.arange(0, MLA_ROPE, 2, dtype=F32) / MLA_ROPE)
    ang = positions.astype(F32).reshape(T, 1) * inv_freq
    cos, sin = jnp.cos(ang), jnp.sin(ang)
    cs = jnp.concatenate([cos, cos, -sin, sin], axis=-1)

    W = RW_WIDTH
    v_first = None
    for l in range(depth):
        w_wide, w_narrow = _pack_in_proj(w_in[l], None if l == 0 else w_vres_down[l - 1])
        g_pre = _row(mix_pre_g[l])
        z_a = norm_matmul(xt, g_pre, w_wide, PROJ_COLS, BF16)
        z_b = norm_matmul(xt, g_pre, w_narrow, w_narrow.shape[1], F32)

        o_hg = hgrn(z_a, z_b, _row(lower_bounds[l]), _row(hgrn_onorm_g[l]), batch, seq)

        mu = rwkv_mu[l]
        p = {
            "mu_r": _row(mu[:W]), "mu_k": _row(mu[W:2 * W]), "mu_v": _row(mu[2 * W:3 * W]),
            "mu_wa": _row(mu[3 * W:3 * W + LANES]), "mu_g": _row(mu[3 * W + LANES:]),
            "w0": _row(rwkv_w0[l]), "w_up": _pad_rows(rwkv_w_up[l], LANES, 0).astype(BF16),
            "a0": _row(rwkv_a0[l]), "a_up": _pad_rows(rwkv_a_up[l], LANES, RW_LORA_W).astype(BF16),
            "g_up": rwkv_g_up[l].astype(BF16),
            "k_k": _row(rwkv_k_k[l]), "k_a": _row(rwkv_k_a[l]),
        }
        if l > 0:
            p["mu_zv"] = _row(jnp.pad(rwkv_vres_mu[l - 1], (0, LANES - RW_LORA_V)))
            p["v0"] = _row(rwkv_v0[l - 1])
            p["v_up"] = _pad_rows(rwkv_v_up[l - 1], LANES, 0).astype(BF16)
        r, ld, k, v, kk, a, g = rw_prep(z_a, z_b, v_first, p, seq)
        if l == 0:
            v_first = v
        o_rw = rw_chunk(r, ld, k, v, kk, a, g, _row(rwkv_r_k[l]), _row(rwkv_lnx_g[l]), _row(rwkv_lnx_b[l]),
                        batch, seq)

        q, kn, vt, krd = mla_up(z_b, cs, _row(mla_q_norm_g[l]), _row(mla_kv_norm_g[l]), _pack_wq(mla_w_uq[l]),
                                mla_w_ukv[l].astype(BF16))
        o_mla = mla_attn(q, kn, vt, krd, batch, seq)

        xt = merge(xt, o_hg, o_rw, o_mla, z_a, w_branch[l].astype(BF16), w_out[l].astype(BF16),
                   _row(mix_post_g[l]))
        xt = ffn(xt, _row(ffn_pre_g[l]), w_ffn_in[l].astype(BF16), w_ffn_out[l].astype(BF16),
                 _row(ffn_post_g[l]))
    return xt.reshape(batch, seq, d)
```

```python
import functools
import itertools
import math

import jax
import jax.numpy as jnp
import numpy as np
from jax import lax
from jax.experimental import pallas as pl
from jax.experimental.pallas import tpu as pltpu

F32 = jnp.float32
BF16 = jnp.bfloat16
LOG2E = math.log2(math.e)

D_MODEL = 1024
NORM_EPS = 1e-6
MASK_VALUE = -1e30
MIN_GATE = 1e-30
HG_HEADS = 8
HG_DK = 128
RW_HEADS = 16
RW_N = 64
RW_WIDTH = RW_HEADS * RW_N
RW_LORA_W = 64
RW_LORA_A = 64
RW_LORA_G = 128
RW_LORA_V = 32
RW_LNX_EPS = 1e-5 * RW_N
MLA_HEADS = 8
MLA_Q_RANK = 384
MLA_KV_RANK = 256
MLA_NOPE = 128
MLA_ROPE = 64
MLA_V = 128
ROPE_THETA = 10000.0
D_FF = 2816

LANES = 128
SUBLANES = 8
VMEM_LIMIT = 48 * 1024 * 1024

PROJ_ROWS = 1024
PROJ_COLS = 3072
RW_PREP_ROWS = 256
MLA_UP_ROWS = 512
MERGE_ROWS = 256
FFN_ROWS = 1024

ZA_HG, ZA_RKV, ZA_GATE, ZA_WIDTH = 0, 3072, 6144, 9216
ZB_F, ZB_CKV, ZB_KR, ZB_WA, ZB_CQ, ZB_G, ZB_VRES = 0, 1024, 1280, 1408, 1536, 1920, 2048


def _cparams(*sem):
    return pltpu.CompilerParams(dimension_semantics=sem, vmem_limit_bytes=VMEM_LIMIT)


def _sigmoid(x):
    return 1.0 / (1.0 + jnp.exp(-x))


def _rms(x, g):
    ms = jnp.mean(x * x, axis=-1, keepdims=True)
    return x * lax.rsqrt(ms + NORM_EPS) * g


def _dot(a, b):
    return jnp.dot(a.astype(BF16), b.astype(BF16), preferred_element_type=F32)


def _dot_nt(a, b):
    return lax.dot_general(a.astype(BF16), b.astype(BF16), (((1,), (1,)), ((), ())), preferred_element_type=F32)


def _dot_tn(a, b):
    return lax.dot_general(a.astype(BF16), b.astype(BF16), (((0,), (0,)), ((), ())), preferred_element_type=F32)


def _tril_ones(n):
    row = lax.broadcasted_iota(jnp.int32, (n, n), 0)
    col = lax.broadcasted_iota(jnp.int32, (n, n), 1)
    return jnp.where(col <= row, 1.0, 0.0).astype(BF16)


def _cumsum_rows(tril, x):
    n = x.shape[1]
    hi = x.astype(BF16)
    r1 = x - hi.astype(F32)
    mid = r1.astype(BF16)
    lo = (r1 - mid.astype(F32)).astype(BF16)
    y = jnp.dot(tril, jnp.concatenate([hi, mid, lo], axis=1), preferred_element_type=F32)
    return y[:, :n] + y[:, n:2 * n] + y[:, 2 * n:]


def _norm_matmul_kernel(x_ref, g_ref, w_ref, o_ref, h_ref):
    @pl.when(pl.program_id(1) == 0)
    def _():
        h_ref[...] = _rms(x_ref[...], g_ref[...]).astype(BF16)

    o_ref[...] = jnp.dot(h_ref[...], w_ref[...], preferred_element_type=F32).astype(o_ref.dtype)


def norm_matmul(x, g, w, tn, out_dtype):
    T, D = x.shape
    N = w.shape[1]
    tm = PROJ_ROWS
    return pl.pallas_call(
        _norm_matmul_kernel,
        grid=(T // tm, N // tn),
        in_specs=[
            pl.BlockSpec((tm, D), lambda i, j: (i, 0)),
            pl.BlockSpec((1, D), lambda i, j: (0, 0)),
            pl.BlockSpec((D, tn), lambda i, j: (0, j)),
        ],
        out_specs=pl.BlockSpec((tm, tn), lambda i, j: (i, j)),
        out_shape=jax.ShapeDtypeStruct((T, N), out_dtype),
        scratch_shapes=[pltpu.VMEM((tm, D), BF16)],
        compiler_params=_cparams("parallel", "arbitrary"),
        name="norm_matmul",
    )(x, g, w)


HG_CHUNK = 128


def _hgrn_term_codes(C):
    t = np.arange(C)[:, None]
    s = np.arange(C)[None, :]
    codes = np.where(s == t, 0, -1)
    m, i = 1, 0
    while m < C:
        hit = ((t // m) % 2 == 1) & (s // m == t // m - 1)
        codes = np.where(hit, 1 + i, codes)
        m, i = 2 * m, i + 1
    return codes.astype(np.int32)


def _hgrn_stages(q_ref, f_ref, v_ref, og_ref, lb_ref, gn_ref, code_ref, o_ref, st_ref):
    C = q_ref.shape[0]
    heads = range(q_ref.shape[1] // HG_DK)

    def hs(ref, h):
        return ref[:, h * HG_DK:(h + 1) * HG_DK].astype(F32)

    code = code_ref[...]
    tril = _tril_ones(C)
    ones = jnp.ones((HG_DK, C), BF16)
    g8 = C // SUBLANES
    rowc = lax.broadcasted_iota(jnp.int32, (C, HG_DK), 0)
    sub = rowc & (SUBLANES - 1)

    def boundary(bh, m):
        if m >= SUBLANES:
            pieces = [jnp.broadcast_to(bh[r0:r0 + 1, :], (2 * m, HG_DK)) for r0 in range(m - 1, C, 2 * m)]
            return pieces[0] if len(pieces) == 1 else jnp.concatenate(pieces, axis=0)
        if m == 1:
            return jnp.where((sub & 1) == 0, bh, pltpu.roll(bh, shift=1, axis=0))
        b3 = bh.reshape(g8, SUBLANES, HG_DK)

        def tile_row(r):
            return jnp.broadcast_to(b3[:, r:r + 1, :], (g8, SUBLANES, HG_DK)).reshape(C, HG_DK)

        if m == 2:
            return jnp.where(sub < 4, tile_row(1), tile_row(5))
        return tile_row(3)

    q = [hs(q_ref, h) for h in heads]
    zf = [hs(f_ref, h) for h in heads]
    v = [hs(v_ref, h) for h in heads]
    lb = [hs(lb_ref, h) for h in heads]
    lf = [jnp.log2(jnp.maximum(lb[h] + (1.0 - lb[h]) * _sigmoid(zf[h]), MIN_GATE)) for h in heads]
    k = [(1.0 - lb[h]) * _sigmoid(-zf[h]) for h in heads]
    b = [_cumsum_rows(tril, lf[h]) for h in heads]
    yield

    c0 = [jnp.dot((q[h] * k[h]).astype(BF16), ones, preferred_element_type=F32) for h in heads]
    scores = [jnp.where(code == 0, c0[h], 0.0) for h in heads]
    yield
    m, i = 1, 0
    while m < C:
        odd = ((rowc >> i) & 1) == 1
        sign = jnp.where(odd, 1.0, -1.0)
        hit = code == 1 + i
        xm = []
        for h in heads:
            d = (b[h] - boundary(b[h], m)) * sign
            xm.append((jnp.where(odd, q[h], k[h]) * jnp.exp2(d)).astype(BF16))
        gm = [_dot_nt(xm[h], xm[h]) for h in heads]
        scores = [jnp.where(hit, gm[h], scores[h]) for h in heads]
        yield
        m, i = 2 * m, i + 1

    st = [st_ref[h] for h in heads]
    o = [_dot(scores[h], v[h]) + _dot_nt(q[h] * jnp.exp2(b[h]), st[h]) for h in heads]
    yield
    bl = [b[h][C - 1:C, :] for h in heads]
    for h in heads:
        st_ref[h] = st[h] * jnp.exp2(bl[h]) + _dot_tn(v[h], k[h] * jnp.exp2(bl[h] - b[h]))
    yield
    for h in heads:
        og = hs(og_ref, h)
        o_ref[:, h * HG_DK:(h + 1) * HG_DK] = (_rms(o[h], gn_ref[...]) * (og * _sigmoid(og))).astype(o_ref.dtype)


def _head_seg(width):
    r = lax.broadcasted_iota(jnp.int32, (width, width), 0)
    c = lax.broadcasted_iota(jnp.int32, (width, width), 1)
    lg = RW_N.bit_length() - 1
    seg = jnp.where((r >> lg) == (c >> lg), 1.0, 0.0).astype(BF16)
    return jnp.concatenate([seg, seg], axis=0)


def _head_sum(x, seg):
    hi = x.astype(BF16)
    lo = (x - hi.astype(F32)).astype(BF16)
    return jnp.dot(jnp.concatenate([hi, lo], axis=1), seg, preferred_element_type=F32)


PREV_ROWS = 16


def _shift_mix(cur_ref, prev_ref, mu, first):
    z = cur_ref[...].astype(F32)
    last = prev_ref[PREV_ROWS - 1:PREV_ROWS, :].astype(F32)
    last = jnp.where(first, jnp.zeros_like(last), last)
    rolled = pltpu.roll(z, shift=1, axis=0)
    rid = lax.broadcasted_iota(jnp.int32, z.shape, 0)
    zp = jnp.where(rid == 0, jnp.broadcast_to(last, z.shape), rolled)
    return z + (zp - z) * mu


def _rw_prep_kernel(has_vres, seq, *refs):
    if has_vres:
        (r_ref, rp_ref, k_ref, kp_ref, v_ref, vp_ref, wa_ref, wap_ref, g_ref, gp_ref, zv_ref, zvp_ref, vf_ref,
         mu_r, mu_k, mu_v, mu_wa, mu_g, mu_zv, w0, w_up, a0, a_up, g_up, v0, v_up, k_k, k_a,
         ro, ldo, ko, vo, kko, ao, go) = refs
    else:
        (r_ref, rp_ref, k_ref, kp_ref, v_ref, vp_ref, wa_ref, wap_ref, g_ref, gp_ref,
         mu_r, mu_k, mu_v, mu_wa, mu_g, w0, w_up, a0, a_up, g_up, k_k, k_a,
         ro, ldo, ko, vo, kko, ao, go) = refs
    tm = r_ref.shape[0]
    first = (pl.program_id(0) * tm) % seq == 0

    r = _shift_mix(r_ref, rp_ref, mu_r[...], first)
    k = _shift_mix(k_ref, kp_ref, mu_k[...], first)
    v = _shift_mix(v_ref, vp_ref, mu_v[...], first)
    wa = _shift_mix(wa_ref, wap_ref, mu_wa[...], first)
    zg = _shift_mix(g_ref, gp_ref, mu_g[...], first)

    zw = w0[...] + _dot(jnp.tanh(wa), w_up[...])
    w_log = -(jnp.maximum(-zw, 0.0) + jnp.log(1.0 + jnp.exp(-jnp.abs(zw)))) - 0.5
    ldo[...] = -jnp.exp(w_log)
    a = _sigmoid(a0[...] + _dot(wa, a_up[...]))
    go[...] = _dot(_sigmoid(zg), g_up[...]).astype(go.dtype)
    if has_vres:
        zv = _shift_mix(zv_ref, zvp_ref, mu_zv[...], first)
        v = v + (vf_ref[...].astype(F32) - v) * _sigmoid(v0[...] + _dot(zv, v_up[...]))
    kk = k * k_k[...]
    seg = _head_seg(LANES)
    for p in range(RW_WIDTH // LANES):
        sl = slice(p * LANES, (p + 1) * LANES)
        kkp = kk[:, sl]
        ss = _head_sum(kkp * kkp, seg)
        kko[:, sl] = (kkp * lax.rsqrt(jnp.maximum(ss, 1e-24))).astype(kko.dtype)
    ro[...] = r.astype(ro.dtype)
    ko[...] = (k * (1.0 + (a - 1.0) * k_a[...])).astype(ko.dtype)
    vo[...] = v.astype(vo.dtype)
    ao[...] = a.astype(ao.dtype)


def rw_prep(z_a, z_b, v_first, p, seq):
    T = z_a.shape[0]
    has_vres = v_first is not None
    W = RW_WIDTH
    tm = RW_PREP_ROWS
    tb = tm // PREV_ROWS

    def cur(width, off):
        return pl.BlockSpec((tm, width), lambda i: (i, off // width))

    def prev(width, off):
        return pl.BlockSpec((PREV_ROWS, width), lambda i: (jnp.maximum(i * tb - 1, 0), off // width))

    def full(a):
        return pl.BlockSpec(a.shape, lambda i: (0,) * a.ndim)

    acts = [z_a, z_a, z_a, z_a, z_a, z_a, z_b, z_b, z_b, z_b]
    specs = [cur(W, ZA_RKV), prev(W, ZA_RKV), cur(W, ZA_RKV + W), prev(W, ZA_RKV + W),
             cur(W, ZA_RKV + 2 * W), prev(W, ZA_RKV + 2 * W),
             cur(LANES, ZB_WA), prev(LANES, ZB_WA), cur(LANES, ZB_G), prev(LANES, ZB_G)]
    if has_vres:
        acts += [z_b, z_b, v_first]
        specs += [cur(LANES, ZB_VRES), prev(LANES, ZB_VRES), pl.BlockSpec((tm, W), lambda i: (i, 0))]
        names = ["mu_r", "mu_k", "mu_v", "mu_wa", "mu_g", "mu_zv", "w0", "w_up", "a0", "a_up", "g_up", "v0", "v_up",
                 "k_k", "k_a"]
    else:
        names = ["mu_r", "mu_k", "mu_v", "mu_wa", "mu_g", "w0", "w_up", "a0", "a_up", "g_up", "k_k", "k_a"]
    params = [p[n] for n in names]
    out_spec = pl.BlockSpec((tm, W), lambda i: (i, 0))
    return pl.pallas_call(
        functools.partial(_rw_prep_kernel, has_vres, seq),
        grid=(T // tm,),
        in_specs=specs + [full(a) for a in params],
        out_specs=[out_spec] * 7,
        out_shape=[jax.ShapeDtypeStruct((T, W), F32 if i == 1 else BF16) for i in range(7)],
        compiler_params=_cparams("parallel"),
        name="rw_prep",
    )(*acts, *params)


RW_STAGES_PER_HGRN_STAGE = 2
RW_CHUNK = 64


def _rw_chunk_stages(r_ref, ld_ref, k_ref, v_ref, kk_ref, a_ref, g_ref, rk_ref, lg_ref, lb_ref, o_ref, st_ref, row0):
    C = RW_CHUNK
    C2 = 2 * C
    pairs = range(r_ref.shape[1] // LANES)
    rows = slice(row0, row0 + C)

    def ps(ref, p):
        return ref[rows if ref.shape[0] > 1 else slice(None), p * LANES:(p + 1) * LANES].astype(F32)

    tril = _tril_ones(C)
    seg = _head_seg(LANES)
    lane = lax.broadcasted_iota(jnp.int32, (C, LANES), 1)
    row = lax.broadcasted_iota(jnp.int32, (C2, C2), 0)
    col = lax.broadcasted_iota(jnp.int32, (C2, C2), 1)
    strict = (col & (C - 1)) < (row & (C - 1))
    incl = (col & (C - 1)) <= (row & (C - 1))
    eye = jnp.where(row == col, 1.0, 0.0)

    def stack(x):
        return jnp.concatenate([jnp.where(lane < RW_N, x, 0.0), jnp.where(lane >= RW_N, x, 0.0)], axis=0).astype(BF16)

    r = [ps(r_ref, p) for p in pairs]
    ld = [ps(ld_ref, p) for p in pairs]
    k = [ps(k_ref, p) for p in pairs]
    v = [ps(v_ref, p) for p in pairs]
    kk = [ps(kk_ref, p) for p in pairs]
    kka = [kk[p] * ps(a_ref, p) for p in pairs]
    cl_all = _cumsum_rows(tril, ld_ref[rows, :])
    cl = [cl_all[:, p * LANES:(p + 1) * LANES] for p in pairs]
    gc = [cl[p][C - 1:C, :] for p in pairs]
    e_inv = [jnp.exp(-cl[p]) for p in pairs]
    e_end = [jnp.exp(gc[p] - cl[p]) for p in pairs]
    ar = [jnp.concatenate([stack(-kk[p] * jnp.exp(cl[p] - ld[p])), stack(r[p] * jnp.exp(cl[p]))], axis=0)
          for p in pairs]
    bk = [jnp.concatenate([stack(kka[p] * e_inv[p]), stack(k[p] * e_inv[p])], axis=0) for p in pairs]
    bkg = [jnp.concatenate([stack(kka[p] * e_end[p]), stack(k[p] * e_end[p])], axis=0) for p in pairs]
    vv = [stack(v[p]) for p in pairs]
    yield

    g4 = [_dot_nt(ar[p], bk[p]) for p in pairs]
    lab = [jnp.where(strict, g4[p][:C2, :C2], 0.0) for p in pairs]
    lkm = [jnp.concatenate([jnp.where(strict, g4[p][:C2, C2:], 0.0), jnp.where(incl, g4[p][C2:, C2:], 0.0)],
                           axis=0).astype(BF16) for p in pairs]
    mb = [jnp.where(incl, g4[p][C2:, :C2], 0.0).astype(BF16) for p in pairs]
    yield
    qi = [eye + lab[p] for p in pairs]
    mp = [_dot(lab[p], lab[p]) for p in pairs]
    yield
    n = 2
    while 2 * n < C:
        mq = [_dot(mp[p], jnp.concatenate([mp[p].astype(BF16), qi[p].astype(BF16)], axis=1)) for p in pairs]
        mp = [mq[p][:, :C2] for p in pairs]
        qi = [qi[p] + mq[p][:, C2:] for p in pairs]
        yield
        n *= 2
    qi = [qi[p] + _dot(mp[p], qi[p]) for p in pairs]
    yield

    zt = [st_ref[p] for p in pairs]
    fz = [_dot(jnp.concatenate([ar[p], lkm[p]], axis=1), jnp.concatenate([zt[p].T.astype(BF16), vv[p]], axis=0))
          for p in pairs]
    yield
    uu = [_dot(qi[p], fz[p][:C2]) for p in pairs]
    yield
    yy = [fz[p][C2:] + _dot(mb[p], uu[p]) for p in pairs]
    for p in pairs:
        uv = jnp.concatenate([uu[p].astype(BF16), vv[p]], axis=0)
        st_ref[p] = zt[p] * jnp.exp(gc[p]) + _dot_tn(uv, bkg[p])
    yield

    y = [yy[p][:C] + yy[p][C:] for p in pairs]
    mean = [_head_sum(y[p], seg) * (1.0 / RW_N) for p in pairs]
    d = [y[p] - mean[p] for p in pairs]
    var = [_head_sum(d[p] * d[p], seg) * (1.0 / RW_N) for p in pairs]
    bonus = [_head_sum(r[p] * k[p] * ps(rk_ref, p), seg) * v[p] for p in pairs]
    yield
    for p in pairs:
        yn = d[p] * lax.rsqrt(var[p] + RW_LNX_EPS) * ps(lg_ref, p) + ps(lb_ref, p)
        o_ref[rows, p * LANES:(p + 1) * LANES] = ((yn + bonus[p]) * ps(g_ref, p)).astype(o_ref.dtype)


def _advance(gen, n):
    end = object()
    while True:
        for _ in range(n):
            if next(gen, end) is end:
                return
        yield


def _recurrent_kernel(hq, hf, hv, hog, hlb, hgn, hcode, r, ld, k, v, kk, a, g, rk, lg, lb, o_hg, o_rw, hst, rst):
    @pl.when(pl.program_id(1) == 0)
    def _():
        hst[...] = jnp.zeros_like(hst)
        rst[...] = jnp.zeros_like(rst)

    rw_refs = (r, ld, k, v, kk, a, g, rk, lg, lb, o_rw, rst)
    rw = itertools.chain(*[_rw_chunk_stages(*rw_refs, row0) for row0 in range(0, HG_CHUNK, RW_CHUNK)])
    hg = _hgrn_stages(hq, hf, hv, hog, hlb, hgn, hcode, o_hg, hst)
    for _ in itertools.zip_longest(_advance(rw, RW_STAGES_PER_HGRN_STAGE), hg):
        pass


def recurrent_mixers(z_a, z_b, lower_bound, onorm_g, r, ld, k, v, kk, a, g, r_k, lnx_g, lnx_b, batch, seq):
    T, W = r.shape
    C = HG_CHUNK
    nc = seq // C
    codes = jnp.asarray(_hgrn_term_codes(C))

    def zspec(seg):
        return pl.BlockSpec((C, W), lambda b, c, seg=seg: (b * nc + c, ZA_HG // W + seg))

    act = pl.BlockSpec((C, W), lambda b, c: (b * nc + c, 0))
    par = pl.BlockSpec((1, W), lambda b, c: (0, 0))
    return pl.pallas_call(
        _recurrent_kernel,
        grid=(batch, nc),
        in_specs=[
            zspec(0), pl.BlockSpec((C, W), lambda b, c: (b * nc + c, ZB_F // W)), zspec(1), zspec(2),
            par,
            pl.BlockSpec((1, HG_DK), lambda b, c: (0, 0)),
            pl.BlockSpec((C, C), lambda b, c: (0, 0)),
        ] + [act] * 7 + [par] * 3,
        out_specs=[act, act],
        out_shape=[jax.ShapeDtypeStruct((T, W), BF16)] * 2,
        scratch_shapes=[pltpu.VMEM((HG_HEADS, HG_DK, HG_DK), F32), pltpu.VMEM((W // LANES, LANES, LANES), F32)],
        compiler_params=_cparams("parallel", "arbitrary"),
        name="recurrent_mixers",
    )(z_a, z_b, z_a, z_a, lower_bound, onorm_g, codes, r, ld, k, v, kk, a, g, r_k, lnx_g, lnx_b)


MLA_QK = 2 * LANES
MLA_TQ = 512
MLA_HEADS_PER_STEP = 4
MLA_VT_ROWS = MLA_V + 16


def _mla_up_kernel(cq_ref, ckv_ref, kr_ref, cs_ref, gq_ref, gkv_ref, wq_ref, wkv_ref, q_ref, kn_ref, vt_ref, krd_ref):
    scale = (MLA_NOPE + MLA_ROPE) ** -0.5 * LOG2E
    cs = cs_ref[...]
    q = _dot(_rms(cq_ref[...], gq_ref[...]), wq_ref[...])
    for h in range(MLA_HEADS):
        o = h * MLA_QK
        q_ref[:, o:o + LANES] = (q[:, o:o + LANES] * scale).astype(q_ref.dtype)
        q_ref[:, o + LANES:o + 2 * LANES] = (q[:, o + LANES:o + 2 * LANES] * (cs * scale)).astype(q_ref.dtype)
    kv = _dot(_rms(ckv_ref[...], gkv_ref[...]), wkv_ref[...])
    for h in range(MLA_HEADS):
        o = h * (MLA_NOPE + MLA_V)
        kn_ref[:, h * MLA_NOPE:(h + 1) * MLA_NOPE] = kv[:, o:o + MLA_NOPE].astype(kn_ref.dtype)
        vt_ref[h, 0, :MLA_V, :] = kv[:, o + MLA_NOPE:o + MLA_NOPE + MLA_V].T.astype(vt_ref.dtype)
        vt_ref[h, 0, MLA_V:, :] = jnp.ones((MLA_VT_ROWS - MLA_V, kv.shape[0]), vt_ref.dtype)
    krx = kr_ref[...] * cs
    krd_ref[...] = (krx + pltpu.roll(krx, shift=MLA_ROPE, axis=1)).astype(krd_ref.dtype)


def mla_up(z_b, cs, gq, gkv, wq, wkv):
    T = z_b.shape[0]
    tm = MLA_TQ
    nq = wq.shape[1]

    def full(a):
        return pl.BlockSpec(a.shape, lambda i: (0,) * a.ndim)

    return pl.pallas_call(
        _mla_up_kernel,
        grid=(T // tm,),
        in_specs=[
            pl.BlockSpec((tm, MLA_Q_RANK), lambda i: (i, ZB_CQ // MLA_Q_RANK)),
            pl.BlockSpec((tm, MLA_KV_RANK), lambda i: (i, ZB_CKV // MLA_KV_RANK)),
            pl.BlockSpec((tm, LANES), lambda i: (i, ZB_KR // LANES)),
            pl.BlockSpec((tm, LANES), lambda i: (i, 0)),
            full(gq), full(gkv), full(wq), full(wkv),
        ],
        out_specs=[
            pl.BlockSpec((tm, nq), lambda i: (i, 0)),
            pl.BlockSpec((tm, MLA_HEADS * MLA_NOPE), lambda i: (i, 0)),
            pl.BlockSpec((MLA_HEADS, 1, MLA_VT_ROWS, tm), lambda i: (0, i, 0, 0)),
            pl.BlockSpec((tm, LANES), lambda i: (i, 0)),
        ],
        out_shape=[
            jax.ShapeDtypeStruct((T, nq), BF16),
            jax.ShapeDtypeStruct((T, MLA_HEADS * MLA_NOPE), BF16),
            jax.ShapeDtypeStruct((MLA_HEADS, T // tm, MLA_VT_ROWS, tm), BF16),
            jax.ShapeDtypeStruct((T, LANES), BF16),
        ],
        compiler_params=_cparams("parallel"),
        name="mla_up",
    )(z_b, z_b, z_b, cs, gq, gkv, wq, wkv)


def _mla_attn_kernel(q_ref, kn_ref, kr_ref, vt_ref, o_ref, m_ref, acc_ref, s_ref):
    tq = q_ref.shape[0]
    heads = range(q_ref.shape[1] // MLA_QK)
    qi = pl.program_id(2)
    q = [q_ref[:, h * MLA_QK:(h + 1) * MLA_QK] for h in heads]
    m_ref[...] = jnp.full_like(m_ref, -jnp.inf)
    acc_ref[...] = jnp.zeros_like(acc_ref)
    row = lax.broadcasted_iota(jnp.int32, (tq, tq), 0)
    col = lax.broadcasted_iota(jnp.int32, (tq, tq), 1)
    causal = row <= col

    def scores(kb):
        ks = pl.multiple_of(kb * tq, tq)
        kr = kr_ref[pl.ds(ks, tq), :]
        kc = [jnp.concatenate([kn_ref[pl.ds(ks, tq), h * MLA_NOPE:(h + 1) * MLA_NOPE], kr], axis=1) for h in heads]
        return [lax.dot_general(kc[h], q[h], (((1,), (1,)), ((), ())), preferred_element_type=F32)
                for h in heads]

    def update(kb):
        s = [s_ref[h] for h in heads]
        m_old = [m_ref[h] for h in heads]
        m_new = [jnp.maximum(m_old[h], jnp.max(s[h], axis=0, keepdims=True)) for h in heads]
        p = [jnp.exp2(s[h] - m_new[h]).astype(BF16) for h in heads]
        for h in heads:
            acc_ref[h] = (jnp.exp2(m_old[h] - m_new[h]) * acc_ref[h]
                          + jnp.dot(vt_ref[h, kb], p[h], preferred_element_type=F32))
            m_ref[h] = m_new[h]

    s0 = scores(0)
    for h in heads:
        s_ref[h] = jnp.where(qi > 0, s0[h], jnp.where(causal, s0[h], MASK_VALUE))

    def body(kb, carry):
        s_next = scores(kb + 1)
        update(kb)
        for h in heads:
            s_ref[h] = s_next[h]
        return carry

    lax.fori_loop(0, qi - 1, body, 0)

    @pl.when(qi > 0)
    def _():
        s_next = scores(qi)
        update(qi - 1)
        for h in heads:
            s_ref[h] = jnp.where(causal, s_next[h], MASK_VALUE)

    update(qi)
    for h in heads:
        acc = acc_ref[h]
        o_ref[:, h * MLA_V:(h + 1) * MLA_V] = (acc[:MLA_V] / acc[MLA_V:MLA_V + 1]).T.astype(o_ref.dtype)


def mla_attn(q, kn, vt, krd, batch, seq):
    T = q.shape[0]
    tq = MLA_TQ
    nq = seq // tq
    g = MLA_HEADS_PER_STEP
    return pl.pallas_call(
        _mla_attn_kernel,
        grid=(batch, MLA_HEADS // g, nq),
        in_specs=[
            pl.BlockSpec((tq, g * MLA_QK), lambda b, h, i: (b * nq + i, h)),
            pl.BlockSpec((seq, g * MLA_NOPE), lambda b, h, i: (b, h)),
            pl.BlockSpec((seq, LANES), lambda b, h, i: (b, 0)),
            pl.BlockSpec((g, nq, MLA_VT_ROWS, tq), lambda b, h, i: (h, b, 0, 0)),
        ],
        out_specs=pl.BlockSpec((tq, g * MLA_V), lambda b, h, i: (b * nq + i, h)),
        out_shape=jax.ShapeDtypeStruct((T, MLA_HEADS * MLA_V), BF16),
        scratch_shapes=[pltpu.VMEM((g, 1, tq), F32), pltpu.VMEM((g, MLA_VT_ROWS, tq), F32),
                        pltpu.VMEM((g, tq, tq), F32)],
        compiler_params=_cparams("parallel", "parallel", "arbitrary"),
        name="mla_attn",
    )(q, kn, krd, vt)


def _merge_kernel(x_ref, ohg_ref, orw_ref, omla_ref, g0_ref, g1_ref, g2_ref, wb_ref, wo_ref, gp_ref, o_ref):
    merged = (_sigmoid(g0_ref[...].astype(F32)) * jnp.dot(ohg_ref[...], wb_ref[0], preferred_element_type=F32)
              + _sigmoid(g1_ref[...].astype(F32)) * jnp.dot(orw_ref[...], wb_ref[1], preferred_element_type=F32)
              + _sigmoid(g2_ref[...].astype(F32)) * jnp.dot(omla_ref[...], wb_ref[2], preferred_element_type=F32))
    y = _dot(merged, wo_ref[...])
    o_ref[...] = x_ref[...] + _rms(y, gp_ref[...])


def merge(x, o_hg, o_rw, o_mla, z_a, w_branch, w_out, g_post):
    T, D = x.shape
    tm = MERGE_ROWS
    row = pl.BlockSpec((tm, D), lambda i: (i, 0))

    def gate(n):
        return pl.BlockSpec((tm, D), lambda i, n=n: (i, ZA_GATE // D + n))

    def full(a):
        return pl.BlockSpec(a.shape, lambda i: (0,) * a.ndim)

    return pl.pallas_call(
        _merge_kernel,
        grid=(T // tm,),
        in_specs=[row, row, row, row, gate(0), gate(1), gate(2), full(w_branch), full(w_out), full(g_post)],
        out_specs=row,
        out_shape=jax.ShapeDtypeStruct((T, D), F32),
        compiler_params=_cparams("parallel"),
        name="merge",
    )(x, o_hg, o_rw, o_mla, z_a, z_a, z_a, w_branch, w_out, g_post)


FFN_TILE = 256


def _ffn_kernel(x_ref, gpre_ref, wg_ref, wu_ref, wo_ref, gpost_ref, o_ref, h_ref, acc_ref):
    j = pl.program_id(1)

    @pl.when(j == 0)
    def _():
        h_ref[...] = _rms(x_ref[...], gpre_ref[...]).astype(BF16)
        acc_ref[...] = jnp.zeros_like(acc_ref)

    h = h_ref[...]
    gate = jnp.dot(h, wg_ref[...], preferred_element_type=F32)
    up = jnp.dot(h, wu_ref[...], preferred_element_type=F32)
    act = gate * _sigmoid(gate) * up
    acc_ref[...] += _dot(act, wo_ref[...])

    @pl.when(j == pl.num_programs(1) - 1)
    def _():
        o_ref[...] = x_ref[...] + _rms(acc_ref[...], gpost_ref[...])


def ffn(x, g_pre, w_in, w_out, g_post):
    T, D = x.shape
    tm = FFN_ROWS
    tf = FFN_TILE
    nf = D_FF // tf
    return pl.pallas_call(
        _ffn_kernel,
        grid=(T // tm, nf),
        in_specs=[
            pl.BlockSpec((tm, D), lambda i, j: (i, 0)),
            pl.BlockSpec((1, D), lambda i, j: (0, 0)),
            pl.BlockSpec((D, tf), lambda i, j: (0, j)),
            pl.BlockSpec((D, tf), lambda i, j: (0, nf + j)),
            pl.BlockSpec((tf, D), lambda i, j: (j, 0)),
            pl.BlockSpec((1, D), lambda i, j: (0, 0)),
        ],
        out_specs=pl.BlockSpec((tm, D), lambda i, j: (i, 0)),
        out_shape=jax.ShapeDtypeStruct((T, D), F32),
        scratch_shapes=[pltpu.VMEM((tm, D), BF16), pltpu.VMEM((tm, D), F32)],
        compiler_params=_cparams("parallel", "arbitrary"),
        name="ffn",
    )(x, g_pre, w_in, w_in, w_out, g_post)


def _swap_halves(w):
    h = w.shape[-1] // 2
    return jnp.concatenate([w[..., h:], w[..., :h]], axis=-1)


def _pad_rows(w, rows, at):
    out = jnp.zeros((rows, w.shape[1]), w.dtype)
    return out.at[at:at + w.shape[0]].set(w)


def _pack_in_proj(w, w_vres):
    hg, rkv, wa, g, cq, ckv, kr, gates = jnp.split(
        w, np.cumsum([4096, 3072, RW_LORA_W + RW_LORA_A, RW_LORA_G, MLA_Q_RANK, MLA_KV_RANK, MLA_ROPE]).tolist(),
        axis=1)
    hq, hf, hi, hog = jnp.split(hg, 4, axis=1)
    wide = jnp.concatenate([hq, hi, hog, rkv, gates], axis=1)
    parts = [hf, ckv, kr, _swap_halves(kr), wa, cq, g]
    if w_vres is not None:
        parts.append(jnp.pad(w_vres, ((0, 0), (0, LANES - RW_LORA_V))))
    narrow = jnp.concatenate(parts, axis=1)
    return wide.astype(BF16), narrow.astype(BF16)


def _pack_wq(w_uq):
    w = w_uq.reshape(MLA_Q_RANK, MLA_HEADS, MLA_NOPE + MLA_ROPE)
    rope = w[..., MLA_NOPE:]
    w = jnp.concatenate([w[..., :MLA_NOPE], rope, _swap_halves(rope)], axis=-1)
    return w.reshape(MLA_Q_RANK, MLA_HEADS * MLA_QK).astype(BF16)


def _row(v):
    return v.reshape(1, -1).astype(F32)


def kernel(x, positions, hgrn_lb_logits, mix_pre_g, mix_post_g, ffn_pre_g, ffn_post_g, w_in, w_vres_down,
           hgrn_onorm_g, rwkv_mu, rwkv_vres_mu, rwkv_w0, rwkv_w_up, rwkv_a0, rwkv_a_up, rwkv_g_up, rwkv_v0,
           rwkv_v_up, rwkv_k_k, rwkv_k_a, rwkv_r_k, rwkv_lnx_g, rwkv_lnx_b, mla_q_norm_g, mla_w_uq,
           mla_kv_norm_g, mla_w_ukv, w_branch, w_out, w_ffn_in, w_ffn_out):
    batch, seq, d = x.shape
    depth = w_in.shape[0]
    T = batch * seq
    xt = x.reshape(T, d)

    probs = jax.nn.softmax(hgrn_lb_logits.astype(F32), axis=0)
    lower_bounds = jnp.cumsum(probs, axis=0) - probs[0]
    inv_freq = ROPE_THETA ** (-jnp.arange(0, MLA_ROPE, 2, dtype=F32) / MLA_ROPE)
    ang = positions.astype(F32).reshape(T, 1) * inv_freq
    cos, sin = jnp.cos(ang), jnp.sin(ang)
    cs = jnp.concatenate([cos, cos, -sin, sin], axis=-1)

    W = RW_WIDTH
    v_first = None
    for l in range(depth):
        w_wide, w_narrow = _pack_in_proj(w_in[l], None if l == 0 else w_vres_down[l - 1])
        g_pre = _row(mix_pre_g[l])
        z_a = norm_matmul(xt, g_pre, w_wide, PROJ_COLS, BF16)
        z_b = norm_matmul(xt, g_pre, w_narrow, w_narrow.shape[1], F32)

        mu = rwkv_mu[l]
        p = {
            "mu_r": _row(mu[:W]), "mu_k": _row(mu[W:2 * W]), "mu_v": _row(mu[2 * W:3 * W]),
            "mu_wa": _row(mu[3 * W:3 * W + LANES]), "mu_g": _row(mu[3 * W + LANES:]),
            "w0": _row(rwkv_w0[l]), "w_up": _pad_rows(rwkv_w_up[l], LANES, 0).astype(BF16),
            "a0": _row(rwkv_a0[l]), "a_up": _pad_rows(rwkv_a_up[l], LANES, RW_LORA_W).astype(BF16),
            "g_up": rwkv_g_up[l].astype(BF16),
            "k_k": _row(rwkv_k_k[l]), "k_a": _row(rwkv_k_a[l]),
        }
        if l > 0:
            p["mu_zv"] = _row(jnp.pad(rwkv_vres_mu[l - 1], (0, LANES - RW_LORA_V)))
            p["v0"] = _row(rwkv_v0[l - 1])
            p["v_up"] = _pad_rows(rwkv_v_up[l - 1], LANES, 0).astype(BF16)
        r, ld, k, v, kk, a, g = rw_prep(z_a, z_b, v_first, p, seq)
        if l == 0:
            v_first = v
        o_hg, o_rw = recurrent_mixers(z_a, z_b, _row(lower_bounds[l]), _row(hgrn_onorm_g[l]), r, ld, k, v, kk, a, g,
                                      _row(rwkv_r_k[l]), _row(rwkv_lnx_g[l]), _row(rwkv_lnx_b[l]), batch, seq)

        q, kn, vt, krd = mla_up(z_b, cs, _row(mla_q_norm_g[l]), _row(mla_kv_norm_g[l]), _pack_wq(mla_w_uq[l]),
                                mla_w_ukv[l].astype(BF16))
        o_mla = mla_attn(q, kn, vt, krd, batch, seq)

        xt = merge(xt, o_hg, o_rw, o_mla, z_a, w_branch[l].astype(BF16), w_out[l].astype(BF16),
                   _row(mix_post_g[l]))
        xt = ffn(xt, _row(ffn_pre_g[l]), w_ffn_in[l].astype(BF16), w_ffn_out[l].astype(BF16),
                 _row(ffn_post_g[l]))
    return xt.reshape(batch, seq, d)
```

```python
import functools
import itertools
import math

import jax
import jax.numpy as jnp
import numpy as np
from jax import lax
from jax.experimental import pallas as pl
from jax.experimental.pallas import tpu as pltpu

F32 = jnp.float32
BF16 = jnp.bfloat16
LOG2E = math.log2(math.e)

D_MODEL = 1024
NORM_EPS = 1e-6
MASK_VALUE = -1e30
MIN_GATE = 1e-30
HG_HEADS = 8
HG_DK = 128
RW_HEADS = 16
RW_N = 64
RW_WIDTH = RW_HEADS * RW_N
RW_LORA_W = 64
RW_LORA_A = 64
RW_LORA_G = 128
RW_LORA_V = 32
RW_LNX_EPS = 1e-5 * RW_N
MLA_HEADS = 8
MLA_Q_RANK = 384
MLA_KV_RANK = 256
MLA_NOPE = 128
MLA_ROPE = 64
MLA_V = 128
ROPE_THETA = 10000.0
D_FF = 2816

LANES = 128
SUBLANES = 8
VMEM_LIMIT = 48 * 1024 * 1024

PROJ_ROWS = 1024
PROJ_COLS = 3072
RW_PREP_ROWS = 256
MLA_UP_ROWS = 512
MERGE_ROWS = 256
FFN_ROWS = 1024

ZA_HG, ZA_RKV, ZA_GATE, ZA_WIDTH = 0, 3072, 6144, 9216
ZB_F, ZB_CKV, ZB_KR, ZB_WA, ZB_CQ, ZB_G, ZB_VRES = 0, 1024, 1280, 1408, 1536, 1920, 2048


def _cparams(*sem):
    return pltpu.CompilerParams(dimension_semantics=sem, vmem_limit_bytes=VMEM_LIMIT)


def _sigmoid(x):
    return 1.0 / (1.0 + jnp.exp(-x))


def _rms(x, g):
    ms = jnp.mean(x * x, axis=-1, keepdims=True)
    return x * lax.rsqrt(ms + NORM_EPS) * g


def _dot(a, b):
    return jnp.dot(a.astype(BF16), b.astype(BF16), preferred_element_type=F32)


def _dot_nt(a, b):
    return lax.dot_general(a.astype(BF16), b.astype(BF16), (((1,), (1,)), ((), ())), preferred_element_type=F32)


def _dot_tn(a, b):
    return lax.dot_general(a.astype(BF16), b.astype(BF16), (((0,), (0,)), ((), ())), preferred_element_type=F32)


def _tril_ones(n):
    row = lax.broadcasted_iota(jnp.int32, (n, n), 0)
    col = lax.broadcasted_iota(jnp.int32, (n, n), 1)
    return jnp.where(col <= row, 1.0, 0.0).astype(BF16)


def _cumsum_rows(tril, x):
    hi = x.astype(BF16)
    r1 = x - hi.astype(F32)
    mid = r1.astype(BF16)
    lo = (r1 - mid.astype(F32)).astype(BF16)
    return jnp.dot(jnp.concatenate([tril, tril, tril], axis=1), jnp.concatenate([hi, mid, lo], axis=0),
                   preferred_element_type=F32)


def _norm_matmul_kernel(x_ref, g_ref, w_ref, o_ref, h_ref):
    @pl.when(pl.program_id(1) == 0)
    def _():
        h_ref[...] = _rms(x_ref[...], g_ref[...]).astype(BF16)

    o_ref[...] = jnp.dot(h_ref[...], w_ref[...], preferred_element_type=F32).astype(o_ref.dtype)


def norm_matmul(x, g, w, tn, out_dtype):
    T, D = x.shape
    N = w.shape[1]
    tm = PROJ_ROWS
    return pl.pallas_call(
        _norm_matmul_kernel,
        grid=(T // tm, N // tn),
        in_specs=[
            pl.BlockSpec((tm, D), lambda i, j: (i, 0)),
            pl.BlockSpec((1, D), lambda i, j: (0, 0)),
            pl.BlockSpec((D, tn), lambda i, j: (0, j)),
        ],
        out_specs=pl.BlockSpec((tm, tn), lambda i, j: (i, j)),
        out_shape=jax.ShapeDtypeStruct((T, N), out_dtype),
        scratch_shapes=[pltpu.VMEM((tm, D), BF16)],
        compiler_params=_cparams("parallel", "arbitrary"),
        name="norm_matmul",
    )(x, g, w)


HG_CHUNK = 128


def _hgrn_term_codes(C):
    t = np.arange(C)[:, None]
    s = np.arange(C)[None, :]
    codes = np.where(s == t, 0, -1)
    m, i = 1, 0
    while m < C:
        hit = ((t // m) % 2 == 1) & (s // m == t // m - 1)
        codes = np.where(hit, 1 + i, codes)
        m, i = 2 * m, i + 1
    return codes.astype(np.int32)


def _hgrn_stages(q_ref, f_ref, v_ref, og_ref, lb_ref, gn_ref, code_ref, o_ref, st_ref, heads):
    C = q_ref.shape[0]

    def hs(ref, h):
        return ref[:, h * HG_DK:(h + 1) * HG_DK].astype(F32)

    code = code_ref[...]
    tril = _tril_ones(C)
    g8 = C // SUBLANES
    rowc = lax.broadcasted_iota(jnp.int32, (C, HG_DK), 0)
    sub = rowc & (SUBLANES - 1)

    def boundary(bh, m):
        if m >= SUBLANES:
            pieces = [jnp.broadcast_to(bh[r0:r0 + 1, :], (2 * m, HG_DK)) for r0 in range(m - 1, C, 2 * m)]
            return pieces[0] if len(pieces) == 1 else jnp.concatenate(pieces, axis=0)
        if m == 1:
            return jnp.where((sub & 1) == 0, bh, pltpu.roll(bh, shift=1, axis=0))
        b3 = bh.reshape(g8, SUBLANES, HG_DK)

        def tile_row(r):
            return jnp.broadcast_to(b3[:, r:r + 1, :], (g8, SUBLANES, HG_DK)).reshape(C, HG_DK)

        if m == 2:
            return jnp.where(sub < 4, tile_row(1), tile_row(5))
        return tile_row(3)

    q = {h: hs(q_ref, h) for h in heads}
    zf = {h: hs(f_ref, h) for h in heads}
    v = {h: hs(v_ref, h) for h in heads}
    lb = {h: hs(lb_ref, h) for h in heads}
    lf = {h: jnp.log2(jnp.maximum(lb[h] + (1.0 - lb[h]) * _sigmoid(zf[h]), MIN_GATE)) for h in heads}
    k = {h: (1.0 - lb[h]) * _sigmoid(-zf[h]) for h in heads}
    b_all = _cumsum_rows(tril, jnp.concatenate([lf[h] for h in heads], axis=1))
    b = {h: b_all[:, n * HG_DK:(n + 1) * HG_DK] for n, h in enumerate(heads)}
    yield

    scores = {h: jnp.where(code == 0, jnp.sum(q[h] * k[h], axis=-1, keepdims=True), 0.0) for h in heads}
    yield
    m, i = 1, 0
    while m < C:
        odd = ((rowc >> i) & 1) == 1
        sign = jnp.where(odd, 1.0, -1.0)
        hit = code == 1 + i
        xm = {}
        for h in heads:
            d = (b[h] - boundary(b[h], m)) * sign
            xm[h] = (jnp.where(odd, q[h], k[h]) * jnp.exp2(d)).astype(BF16)
        gm = {h: _dot_nt(xm[h], xm[h]) for h in heads}
        scores = {h: jnp.where(hit, gm[h], scores[h]) for h in heads}
        yield
        m, i = 2 * m, i + 1

    st = {h: st_ref[h] for h in heads}
    o = {h: _dot(jnp.concatenate([scores[h].astype(BF16), (q[h] * jnp.exp2(b[h])).astype(BF16)], axis=1),
                 jnp.concatenate([v[h].astype(BF16), st[h].T.astype(BF16)], axis=0)) for h in heads}
    yield
    bl = {h: b[h][C - 1:C, :] for h in heads}
    for h in heads:
        st_ref[h] = st[h] * jnp.exp2(bl[h]) + _dot_tn(v[h], k[h] * jnp.exp2(bl[h] - b[h]))
    yield
    for h in heads:
        og = hs(og_ref, h)
        o_ref[:, h * HG_DK:(h + 1) * HG_DK] = (_rms(o[h], gn_ref[...]) * (og * _sigmoid(og))).astype(o_ref.dtype)


def _head_seg(width):
    r = lax.broadcasted_iota(jnp.int32, (width, width), 0)
    c = lax.broadcasted_iota(jnp.int32, (width, width), 1)
    lg = RW_N.bit_length() - 1
    seg = jnp.where((r >> lg) == (c >> lg), 1.0, 0.0).astype(BF16)
    return jnp.concatenate([seg, seg], axis=0)


def _head_sum(x, seg):
    hi = x.astype(BF16)
    lo = (x - hi.astype(F32)).astype(BF16)
    return jnp.dot(jnp.concatenate([hi, lo], axis=1), seg, preferred_element_type=F32)


PREV_ROWS = 16


def _shift_mix(cur_ref, prev_ref, mu, first):
    z = cur_ref[...].astype(F32)
    last = prev_ref[PREV_ROWS - 1:PREV_ROWS, :].astype(F32)
    last = jnp.where(first, jnp.zeros_like(last), last)
    rolled = pltpu.roll(z, shift=1, axis=0)
    rid = lax.broadcasted_iota(jnp.int32, z.shape, 0)
    zp = jnp.where(rid == 0, jnp.broadcast_to(last, z.shape), rolled)
    return z + (zp - z) * mu


def _rw_prep_kernel(has_vres, seq, *refs):
    if has_vres:
        (r_ref, rp_ref, k_ref, kp_ref, v_ref, vp_ref, wa_ref, wap_ref, g_ref, gp_ref, zv_ref, zvp_ref, vf_ref,
         mu_r, mu_k, mu_v, mu_wa, mu_g, mu_zv, w0, w_up, a0, a_up, g_up, v0, v_up, k_k, k_a,
         ro, ldo, ko, vo, kko, ao, go) = refs
    else:
        (r_ref, rp_ref, k_ref, kp_ref, v_ref, vp_ref, wa_ref, wap_ref, g_ref, gp_ref,
         mu_r, mu_k, mu_v, mu_wa, mu_g, w0, w_up, a0, a_up, g_up, k_k, k_a,
         ro, ldo, ko, vo, kko, ao, go) = refs
    tm = r_ref.shape[0]
    first = (pl.program_id(0) * tm) % seq == 0

    r = _shift_mix(r_ref, rp_ref, mu_r[...], first)
    k = _shift_mix(k_ref, kp_ref, mu_k[...], first)
    v = _shift_mix(v_ref, vp_ref, mu_v[...], first)
    wa = _shift_mix(wa_ref, wap_ref, mu_wa[...], first)
    zg = _shift_mix(g_ref, gp_ref, mu_g[...], first)

    zw = w0[...] + _dot(jnp.tanh(wa), w_up[...])
    w_log = -(jnp.maximum(-zw, 0.0) + jnp.log(1.0 + jnp.exp(-jnp.abs(zw)))) - 0.5
    ldo[...] = -jnp.exp(w_log)
    a = _sigmoid(a0[...] + _dot(wa, a_up[...]))
    go[...] = _dot(_sigmoid(zg), g_up[...]).astype(go.dtype)
    if has_vres:
        zv = _shift_mix(zv_ref, zvp_ref, mu_zv[...], first)
        v = v + (vf_ref[...].astype(F32) - v) * _sigmoid(v0[...] + _dot(zv, v_up[...]))
    kk = k * k_k[...]
    seg = _head_seg(LANES)
    for p in range(RW_WIDTH // LANES):
        sl = slice(p * LANES, (p + 1) * LANES)
        kkp = kk[:, sl]
        ss = _head_sum(kkp * kkp, seg)
        kko[:, sl] = (kkp * lax.rsqrt(jnp.maximum(ss, 1e-24))).astype(kko.dtype)
    ro[...] = r.astype(ro.dtype)
    ko[...] = (k * (1.0 + (a - 1.0) * k_a[...])).astype(ko.dtype)
    vo[...] = v.astype(vo.dtype)
    ao[...] = a.astype(ao.dtype)


def rw_prep(z_a, z_b, v_first, p, seq):
    T = z_a.shape[0]
    has_vres = v_first is not None
    W = RW_WIDTH
    tm = RW_PREP_ROWS
    tb = tm // PREV_ROWS

    def cur(width, off):
        return pl.BlockSpec((tm, width), lambda i: (i, off // width))

    def prev(width, off):
        return pl.BlockSpec((PREV_ROWS, width), lambda i: (jnp.maximum(i * tb - 1, 0), off // width))

    def full(a):
        return pl.BlockSpec(a.shape, lambda i: (0,) * a.ndim)

    acts = [z_a, z_a, z_a, z_a, z_a, z_a, z_b, z_b, z_b, z_b]
    specs = [cur(W, ZA_RKV), prev(W, ZA_RKV), cur(W, ZA_RKV + W), prev(W, ZA_RKV + W),
             cur(W, ZA_RKV + 2 * W), prev(W, ZA_RKV + 2 * W),
             cur(LANES, ZB_WA), prev(LANES, ZB_WA), cur(LANES, ZB_G), prev(LANES, ZB_G)]
    if has_vres:
        acts += [z_b, z_b, v_first]
        specs += [cur(LANES, ZB_VRES), prev(LANES, ZB_VRES), pl.BlockSpec((tm, W), lambda i: (i, 0))]
        names = ["mu_r", "mu_k", "mu_v", "mu_wa", "mu_g", "mu_zv", "w0", "w_up", "a0", "a_up", "g_up", "v0", "v_up",
                 "k_k", "k_a"]
    else:
        names = ["mu_r", "mu_k", "mu_v", "mu_wa", "mu_g", "w0", "w_up", "a0", "a_up", "g_up", "k_k", "k_a"]
    params = [p[n] for n in names]
    out_spec = pl.BlockSpec((tm, W), lambda i: (i, 0))
    return pl.pallas_call(
        functools.partial(_rw_prep_kernel, has_vres, seq),
        grid=(T // tm,),
        in_specs=specs + [full(a) for a in params],
        out_specs=[out_spec] * 7,
        out_shape=[jax.ShapeDtypeStruct((T, W), F32 if i == 1 else BF16) for i in range(7)],
        compiler_params=_cparams("parallel"),
        name="rw_prep",
    )(*acts, *params)


HG_HEADS_PER_PASS = 4
RW_STAGES_PER_STEP = 1
HG_STAGES_PER_STEP = 1
RW_CHUNK = 64


def _rw_chunk_stages(r_ref, ld_ref, k_ref, v_ref, kk_ref, a_ref, g_ref, rk_ref, lg_ref, lb_ref, o_ref, st_ref, row0,
                     pairs):
    C = RW_CHUNK
    C2 = 2 * C
    rows = slice(row0, row0 + C)

    def ps(ref, p):
        return ref[rows if ref.shape[0] > 1 else slice(None), p * LANES:(p + 1) * LANES].astype(F32)

    tril = _tril_ones(C)
    seg = _head_seg(LANES)
    lane = lax.broadcasted_iota(jnp.int32, (C, LANES), 1)
    row = lax.broadcasted_iota(jnp.int32, (C2, C2), 0)
    col = lax.broadcasted_iota(jnp.int32, (C2, C2), 1)
    strict = (col & (C - 1)) < (row & (C - 1))
    incl = (col & (C - 1)) <= (row & (C - 1))
    eye = jnp.where(row == col, 1.0, 0.0)

    def stack(x):
        return jnp.concatenate([jnp.where(lane < RW_N, x, 0.0), jnp.where(lane >= RW_N, x, 0.0)], axis=0).astype(BF16)

    r = {p: ps(r_ref, p) for p in pairs}
    ld = {p: ps(ld_ref, p) for p in pairs}
    k = {p: ps(k_ref, p) for p in pairs}
    v = {p: ps(v_ref, p) for p in pairs}
    kk = {p: ps(kk_ref, p) for p in pairs}
    kka = {p: kk[p] * ps(a_ref, p) for p in pairs}
    cl_all = _cumsum_rows(tril, ld_ref[rows, pairs[0] * LANES:(pairs[-1] + 1) * LANES])
    cl = {p: cl_all[:, n * LANES:(n + 1) * LANES] for n, p in enumerate(pairs)}
    gc = {p: cl[p][C - 1:C, :] for p in pairs}
    e_inv = {p: jnp.exp(-cl[p]) for p in pairs}
    e_end = {p: jnp.exp(gc[p] - cl[p]) for p in pairs}
    ar = {p: jnp.concatenate([stack(-kk[p] * jnp.exp(cl[p] - ld[p])), stack(r[p] * jnp.exp(cl[p]))], axis=0)
          for p in pairs}
    bk = {p: jnp.concatenate([stack(kka[p] * e_inv[p]), stack(k[p] * e_inv[p])], axis=0) for p in pairs}
    bkg = {p: jnp.concatenate([stack(kka[p] * e_end[p]), stack(k[p] * e_end[p])], axis=0) for p in pairs}
    vv = {p: stack(v[p]) for p in pairs}
    yield

    g4 = {p: _dot_nt(ar[p], bk[p]) for p in pairs}
    lab = {p: jnp.where(strict, g4[p][:C2, :C2], 0.0) for p in pairs}
    lkm = {p: jnp.concatenate([jnp.where(strict, g4[p][:C2, C2:], 0.0), jnp.where(incl, g4[p][C2:, C2:], 0.0)],
                              axis=0).astype(BF16) for p in pairs}
    mb = {p: jnp.where(incl, g4[p][C2:, :C2], 0.0).astype(BF16) for p in pairs}
    yield
    qi = {p: eye + lab[p] for p in pairs}
    mp = {p: _dot(lab[p], lab[p]) for p in pairs}
    yield
    n = 2
    while 2 * n < C:
        mq = {p: _dot(mp[p], jnp.concatenate([mp[p].astype(BF16), qi[p].astype(BF16)], axis=1)) for p in pairs}
        mp = {p: mq[p][:, :C2] for p in pairs}
        qi = {p: qi[p] + mq[p][:, C2:] for p in pairs}
        yield
        n *= 2
    qi = {p: qi[p] + _dot(mp[p], qi[p]) for p in pairs}
    yield

    zt = {p: st_ref[p] for p in pairs}
    fz = {p: _dot(jnp.concatenate([ar[p], lkm[p]], axis=1), jnp.concatenate([zt[p].T.astype(BF16), vv[p]], axis=0))
          for p in pairs}
    yield
    uu = {p: _dot(qi[p], fz[p][:C2]) for p in pairs}
    yield
    yy = {p: fz[p][C2:] + _dot(mb[p], uu[p]) for p in pairs}
    for p in pairs:
        uv = jnp.concatenate([uu[p].astype(BF16), vv[p]], axis=0)
        st_ref[p] = zt[p] * jnp.exp(gc[p]) + _dot_tn(uv, bkg[p])
    yield

    y = {p: yy[p][:C] + yy[p][C:] for p in pairs}
    mean = {p: _head_sum(y[p], seg) * (1.0 / RW_N) for p in pairs}
    d = {p: y[p] - mean[p] for p in pairs}
    var = {p: _head_sum(d[p] * d[p], seg) * (1.0 / RW_N) for p in pairs}
    bonus = {p: _head_sum(r[p] * k[p] * ps(rk_ref, p), seg) * v[p] for p in pairs}
    yield
    for p in pairs:
        yn = d[p] * lax.rsqrt(var[p] + RW_LNX_EPS) * ps(lg_ref, p) + ps(lb_ref, p)
        o_ref[rows, p * LANES:(p + 1) * LANES] = ((yn + bonus[p]) * ps(g_ref, p)).astype(o_ref.dtype)


def _advance(gen, n):
    end = object()
    while True:
        for _ in range(n):
            if next(gen, end) is end:
                return
        yield


def _recurrent_kernel(hq, hf, hv, hog, hlb, hgn, hcode, r, ld, k, v, kk, a, g, rk, lg, lb, o_hg, o_rw, hst, rst):
    @pl.when(pl.program_id(1) == 0)
    def _():
        hst[...] = jnp.zeros_like(hst)
        rst[...] = jnp.zeros_like(rst)

    rw_refs = (r, ld, k, v, kk, a, g, rk, lg, lb, o_rw, rst)
    pairs = range(r.shape[1] // LANES)
    rw = itertools.chain(*[_rw_chunk_stages(*rw_refs, row0, pairs) for row0 in range(0, HG_CHUNK, RW_CHUNK)])
    n = HG_HEADS_PER_PASS
    hg = itertools.chain(*[_hgrn_stages(hq, hf, hv, hog, hlb, hgn, hcode, o_hg, hst, range(h0, h0 + n))
                           for h0 in range(0, HG_HEADS, n)])
    for _ in itertools.zip_longest(_advance(rw, RW_STAGES_PER_STEP), _advance(hg, HG_STAGES_PER_STEP)):
        pass


def recurrent_mixers(z_a, z_b, lower_bound, onorm_g, r, ld, k, v, kk, a, g, r_k, lnx_g, lnx_b, batch, seq):
    T, W = r.shape
    C = HG_CHUNK
    nc = seq // C
    codes = jnp.asarray(_hgrn_term_codes(C))

    def zspec(seg):
        return pl.BlockSpec((C, W), lambda b, c, seg=seg: (b * nc + c, ZA_HG // W + seg))

    act = pl.BlockSpec((C, W), lambda b, c: (b * nc + c, 0))
    par = pl.BlockSpec((1, W), lambda b, c: (0, 0))
    return pl.pallas_call(
        _recurrent_kernel,
        grid=(batch, nc),
        in_specs=[
            zspec(0), pl.BlockSpec((C, W), lambda b, c: (b * nc + c, ZB_F // W)), zspec(1), zspec(2),
            par,
            pl.BlockSpec((1, HG_DK), lambda b, c: (0, 0)),
            pl.BlockSpec((C, C), lambda b, c: (0, 0)),
        ] + [act] * 7 + [par] * 3,
        out_specs=[act, act],
        out_shape=[jax.ShapeDtypeStruct((T, W), BF16)] * 2,
        scratch_shapes=[pltpu.VMEM((HG_HEADS, HG_DK, HG_DK), F32), pltpu.VMEM((W // LANES, LANES, LANES), F32)],
        compiler_params=_cparams("parallel", "arbitrary"),
        name="recurrent_mixers",
    )(z_a, z_b, z_a, z_a, lower_bound, onorm_g, codes, r, ld, k, v, kk, a, g, r_k, lnx_g, lnx_b)


MLA_QK = 2 * LANES
MLA_TQ = 512
MLA_HEADS_PER_STEP = 4
MLA_VT_ROWS = MLA_V + 16


def _mla_up_kernel(cq_ref, ckv_ref, kr_ref, cs_ref, gq_ref, gkv_ref, wq_ref, wkv_ref, q_ref, kn_ref, vt_ref, krd_ref):
    scale = (MLA_NOPE + MLA_ROPE) ** -0.5 * LOG2E
    cs = cs_ref[...]
    q = _dot(_rms(cq_ref[...], gq_ref[...]), wq_ref[...])
    for h in range(MLA_HEADS):
        o = h * MLA_QK
        q_ref[:, o:o + LANES] = (q[:, o:o + LANES] * scale).astype(q_ref.dtype)
        q_ref[:, o + LANES:o + 2 * LANES] = (q[:, o + LANES:o + 2 * LANES] * (cs * scale)).astype(q_ref.dtype)
    kv = _dot(_rms(ckv_ref[...], gkv_ref[...]), wkv_ref[...])
    for h in range(MLA_HEADS):
        o = h * (MLA_NOPE + MLA_V)
        kn_ref[:, h * MLA_NOPE:(h + 1) * MLA_NOPE] = kv[:, o:o + MLA_NOPE].astype(kn_ref.dtype)
        vt_ref[h, 0, :MLA_V, :] = kv[:, o + MLA_NOPE:o + MLA_NOPE + MLA_V].T.astype(vt_ref.dtype)
        vt_ref[h, 0, MLA_V:, :] = jnp.ones((MLA_VT_ROWS - MLA_V, kv.shape[0]), vt_ref.dtype)
    krx = kr_ref[...] * cs
    krd_ref[...] = (krx + pltpu.roll(krx, shift=MLA_ROPE, axis=1)).astype(krd_ref.dtype)


def mla_up(z_b, cs, gq, gkv, wq, wkv):
    T = z_b.shape[0]
    tm = MLA_TQ
    nq = wq.shape[1]

    def full(a):
        return pl.BlockSpec(a.shape, lambda i: (0,) * a.ndim)

    return pl.pallas_call(
        _mla_up_kernel,
        grid=(T // tm,),
        in_specs=[
            pl.BlockSpec((tm, MLA_Q_RANK), lambda i: (i, ZB_CQ // MLA_Q_RANK)),
            pl.BlockSpec((tm, MLA_KV_RANK), lambda i: (i, ZB_CKV // MLA_KV_RANK)),
            pl.BlockSpec((tm, LANES), lambda i: (i, ZB_KR // LANES)),
            pl.BlockSpec((tm, LANES), lambda i: (i, 0)),
            full(gq), full(gkv), full(wq), full(wkv),
        ],
        out_specs=[
            pl.BlockSpec((tm, nq), lambda i: (i, 0)),
            pl.BlockSpec((tm, MLA_HEADS * MLA_NOPE), lambda i: (i, 0)),
            pl.BlockSpec((MLA_HEADS, 1, MLA_VT_ROWS, tm), lambda i: (0, i, 0, 0)),
            pl.BlockSpec((tm, LANES), lambda i: (i, 0)),
        ],
        out_shape=[
            jax.ShapeDtypeStruct((T, nq), BF16),
            jax.ShapeDtypeStruct((T, MLA_HEADS * MLA_NOPE), BF16),
            jax.ShapeDtypeStruct((MLA_HEADS, T // tm, MLA_VT_ROWS, tm), BF16),
            jax.ShapeDtypeStruct((T, LANES), BF16),
        ],
        compiler_params=_cparams("parallel"),
        name="mla_up",
    )(z_b, z_b, z_b, cs, gq, gkv, wq, wkv)


def _mla_attn_kernel(q_ref, kn_ref, kr_ref, vt_ref, o_ref, m_ref, acc_ref, s_ref):
    tq = q_ref.shape[0]
    heads = range(q_ref.shape[1] // MLA_QK)
    qi = pl.program_id(2)
    q = [q_ref[:, h * MLA_QK:(h + 1) * MLA_QK] for h in heads]
    m_ref[...] = jnp.full_like(m_ref, -jnp.inf)
    acc_ref[...] = jnp.zeros_like(acc_ref)
    row = lax.broadcasted_iota(jnp.int32, (tq, tq), 0)
    col = lax.broadcasted_iota(jnp.int32, (tq, tq), 1)
    causal = row <= col

    def scores(kb):
        ks = pl.multiple_of(kb * tq, tq)
        kr = kr_ref[pl.ds(ks, tq), :]
        kc = [jnp.concatenate([kn_ref[pl.ds(ks, tq), h * MLA_NOPE:(h + 1) * MLA_NOPE], kr], axis=1) for h in heads]
        return [lax.dot_general(kc[h], q[h], (((1,), (1,)), ((), ())), preferred_element_type=F32)
                for h in heads]

    def update(kb):
        s = [s_ref[h] for h in heads]
        m_old = [m_ref[h] for h in heads]
        m_new = [jnp.maximum(m_old[h], jnp.max(s[h], axis=0, keepdims=True)) for h in heads]
        p = [jnp.exp2(s[h] - m_new[h]).astype(BF16) for h in heads]
        for h in heads:
            acc_ref[h] = (jnp.exp2(m_old[h] - m_new[h]) * acc_ref[h]
                          + jnp.dot(vt_ref[h, kb], p[h], preferred_element_type=F32))
            m_ref[h] = m_new[h]

    s0 = scores(0)
    for h in heads:
        s_ref[h] = jnp.where(qi > 0, s0[h], jnp.where(causal, s0[h], MASK_VALUE))

    def body(kb, carry):
        s_next = scores(kb + 1)
        update(kb)
        for h in heads:
            s_ref[h] = s_next[h]
        return carry

    lax.fori_loop(0, qi - 1, body, 0)

    @pl.when(qi > 0)
    def _():
        s_next = scores(qi)
        update(qi - 1)
        for h in heads:
            s_ref[h] = jnp.where(causal, s_next[h], MASK_VALUE)

    update(qi)
    for h in heads:
        acc = acc_ref[h]
        o_ref[:, h * MLA_V:(h + 1) * MLA_V] = (acc[:MLA_V] / acc[MLA_V:MLA_V + 1]).T.astype(o_ref.dtype)


def mla_attn(q, kn, vt, krd, batch, seq):
    T = q.shape[0]
    tq = MLA_TQ
    nq = seq // tq
    g = MLA_HEADS_PER_STEP
    return pl.pallas_call(
        _mla_attn_kernel,
        grid=(batch, MLA_HEADS // g, nq),
        in_specs=[
            pl.BlockSpec((tq, g * MLA_QK), lambda b, h, i: (b * nq + i, h)),
            pl.BlockSpec((seq, g * MLA_NOPE), lambda b, h, i: (b, h)),
            pl.BlockSpec((seq, LANES), lambda b, h, i: (b, 0)),
            pl.BlockSpec((g, nq, MLA_VT_ROWS, tq), lambda b, h, i: (h, b, 0, 0)),
        ],
        out_specs=pl.BlockSpec((tq, g * MLA_V), lambda b, h, i: (b * nq + i, h)),
        out_shape=jax.ShapeDtypeStruct((T, MLA_HEADS * MLA_V), BF16),
        scratch_shapes=[pltpu.VMEM((g, 1, tq), F32), pltpu.VMEM((g, MLA_VT_ROWS, tq), F32),
                        pltpu.VMEM((g, tq, tq), F32)],
        compiler_params=_cparams("parallel", "parallel", "arbitrary"),
        name="mla_attn",
    )(q, kn, krd, vt)


def _merge_kernel(x_ref, ohg_ref, orw_ref, omla_ref, g0_ref, g1_ref, g2_ref, wb_ref, wo_ref, gp_ref, o_ref):
    merged = (_sigmoid(g0_ref[...].astype(F32)) * jnp.dot(ohg_ref[...], wb_ref[0], preferred_element_type=F32)
              + _sigmoid(g1_ref[...].astype(F32)) * jnp.dot(orw_ref[...], wb_ref[1], preferred_element_type=F32)
              + _sigmoid(g2_ref[...].astype(F32)) * jnp.dot(omla_ref[...], wb_ref[2], preferred_element_type=F32))
    y = _dot(merged, wo_ref[...])
    o_ref[...] = x_ref[...] + _rms(y, gp_ref[...])


def merge(x, o_hg, o_rw, o_mla, z_a, w_branch, w_out, g_post):
    T, D = x.shape
    tm = MERGE_ROWS
    row = pl.BlockSpec((tm, D), lambda i: (i, 0))

    def gate(n):
        return pl.BlockSpec((tm, D), lambda i, n=n: (i, ZA_GATE // D + n))

    def full(a):
        return pl.BlockSpec(a.shape, lambda i: (0,) * a.ndim)

    return pl.pallas_call(
        _merge_kernel,
        grid=(T // tm,),
        in_specs=[row, row, row, row, gate(0), gate(1), gate(2), full(w_branch), full(w_out), full(g_post)],
        out_specs=row,
        out_shape=jax.ShapeDtypeStruct((T, D), F32),
        compiler_params=_cparams("parallel"),
        name="merge",
    )(x, o_hg, o_rw, o_mla, z_a, z_a, z_a, w_branch, w_out, g_post)


FFN_TILE = 256


def _ffn_kernel(x_ref, gpre_ref, wg_ref, wu_ref, wo_ref, gpost_ref, o_ref, h_ref, acc_ref):
    j = pl.program_id(1)

    @pl.when(j == 0)
    def _():
        h_ref[...] = _rms(x_ref[...], gpre_ref[...]).astype(BF16)
        acc_ref[...] = jnp.zeros_like(acc_ref)

    h = h_ref[...]
    gate = jnp.dot(h, wg_ref[...], preferred_element_type=F32)
    up = jnp.dot(h, wu_ref[...], preferred_element_type=F32)
    act = gate * _sigmoid(gate) * up
    acc_ref[...] += _dot(act, wo_ref[...])

    @pl.when(j == pl.num_programs(1) - 1)
    def _():
        o_ref[...] = x_ref[...] + _rms(acc_ref[...], gpost_ref[...])


def ffn(x, g_pre, w_in, w_out, g_post):
    T, D = x.shape
    tm = FFN_ROWS
    tf = FFN_TILE
    nf = D_FF // tf
    return pl.pallas_call(
        _ffn_kernel,
        grid=(T // tm, nf),
        in_specs=[
            pl.BlockSpec((tm, D), lambda i, j: (i, 0)),
            pl.BlockSpec((1, D), lambda i, j: (0, 0)),
            pl.BlockSpec((D, tf), lambda i, j: (0, j)),
            pl.BlockSpec((D, tf), lambda i, j: (0, nf + j)),
            pl.BlockSpec((tf, D), lambda i, j: (j, 0)),
            pl.BlockSpec((1, D), lambda i, j: (0, 0)),
        ],
        out_specs=pl.BlockSpec((tm, D), lambda i, j: (i, 0)),
        out_shape=jax.ShapeDtypeStruct((T, D), F32),
        scratch_shapes=[pltpu.VMEM((tm, D), BF16), pltpu.VMEM((tm, D), F32)],
        compiler_params=_cparams("parallel", "arbitrary"),
        name="ffn",
    )(x, g_pre, w_in, w_in, w_out, g_post)


def _swap_halves(w):
    h = w.shape[-1] // 2
    return jnp.concatenate([w[..., h:], w[..., :h]], axis=-1)


def _pad_rows(w, rows, at):
    out = jnp.zeros((rows, w.shape[1]), w.dtype)
    return out.at[at:at + w.shape[0]].set(w)


def _pack_in_proj(w, w_vres):
    hg, rkv, wa, g, cq, ckv, kr, gates = jnp.split(
        w, np.cumsum([4096, 3072, RW_LORA_W + RW_LORA_A, RW_LORA_G, MLA_Q_RANK, MLA_KV_RANK, MLA_ROPE]).tolist(),
        axis=1)
    hq, hf, hi, hog = jnp.split(hg, 4, axis=1)
    wide = jnp.concatenate([hq, hi, hog, rkv, gates], axis=1)
    parts = [hf, ckv, kr, _swap_halves(kr), wa, cq, g]
    if w_vres is not None:
        parts.append(jnp.pad(w_vres, ((0, 0), (0, LANES - RW_LORA_V))))
    narrow = jnp.concatenate(parts, axis=1)
    return wide.astype(BF16), narrow.astype(BF16)


def _pack_wq(w_uq):
    w = w_uq.reshape(MLA_Q_RANK, MLA_HEADS, MLA_NOPE + MLA_ROPE)
    rope = w[..., MLA_NOPE:]
    w = jnp.concatenate([w[..., :MLA_NOPE], rope, _swap_halves(rope)], axis=-1)
    return w.reshape(MLA_Q_RANK, MLA_HEADS * MLA_QK).astype(BF16)


def _row(v):
    return v.reshape(1, -1).astype(F32)


def kernel(x, positions, hgrn_lb_logits, mix_pre_g, mix_post_g, ffn_pre_g, ffn_post_g, w_in, w_vres_down,
           hgrn_onorm_g, rwkv_mu, rwkv_vres_mu, rwkv_w0, rwkv_w_up, rwkv_a0, rwkv_a_up, rwkv_g_up, rwkv_v0,
           rwkv_v_up, rwkv_k_k, rwkv_k_a, rwkv_r_k, rwkv_lnx_g, rwkv_lnx_b, mla_q_norm_g, mla_w_uq,
           mla_kv_norm_g, mla_w_ukv, w_branch, w_out, w_ffn_in, w_ffn_out):
    batch, seq, d = x.shape
    depth = w_in.shape[0]
    T = batch * seq
    xt = x.reshape(T, d)

    probs = jax.nn.softmax(hgrn_lb_logits.astype(F32), axis=0)
    lower_bounds = jnp.cumsum(probs, axis=0) - probs[0]
    inv_freq = ROPE_THETA ** (-jnp.arange(0, MLA_ROPE, 2, dtype=F32) / MLA_ROPE)
    ang = positions.astype(F32).reshape(T, 1) * inv_freq
    cos, sin = jnp.cos(ang), jnp.sin(ang)
    cs = jnp.concatenate([cos, cos, -sin, sin], axis=-1)

    W = RW_WIDTH
    v_first = None
    for l in range(depth):
        w_wide, w_narrow = _pack_in_proj(w_in[l], None if l == 0 else w_vres_down[l - 1])
        g_pre = _row(mix_pre_g[l])
        z_a = norm_matmul(xt, g_pre, w_wide, PROJ_COLS, BF16)
        z_b = norm_matmul(xt, g_pre, w_narrow, w_narrow.shape[1], F32)

        mu = rwkv_mu[l]
        p = {
            "mu_r": _row(mu[:W]), "mu_k": _row(mu[W:2 * W]), "mu_v": _row(mu[2 * W:3 * W]),
            "mu_wa": _row(mu[3 * W:3 * W + LANES]), "mu_g": _row(mu[3 * W + LANES:]),
            "w0": _row(rwkv_w0[l]), "w_up": _pad_rows(rwkv_w_up[l], LANES, 0).astype(BF16),
            "a0": _row(rwkv_a0[l]), "a_up": _pad_rows(rwkv_a_up[l], LANES, RW_LORA_W).astype(BF16),
            "g_up": rwkv_g_up[l].astype(BF16),
            "k_k": _row(rwkv_k_k[l]), "k_a": _row(rwkv_k_a[l]),
        }
        if l > 0:
            p["mu_zv"] = _row(jnp.pad(rwkv_vres_mu[l - 1], (0, LANES - RW_LORA_V)))
            p["v0"] = _row(rwkv_v0[l - 1])
            p["v_up"] = _pad_rows(rwkv_v_up[l - 1], LANES, 0).astype(BF16)
        r, ld, k, v, kk, a, g = rw_prep(z_a, z_b, v_first, p, seq)
        if l == 0:
            v_first = v
        o_hg, o_rw = recurrent_mixers(z_a, z_b, _row(lower_bounds[l]), _row(hgrn_onorm_g[l]), r, ld, k, v, kk, a, g,
                                      _row(rwkv_r_k[l]), _row(rwkv_lnx_g[l]), _row(rwkv_lnx_b[l]), batch, seq)

        q, kn, vt, krd = mla_up(z_b, cs, _row(mla_q_norm_g[l]), _row(mla_kv_norm_g[l]), _pack_wq(mla_w_uq[l]),
                                mla_w_ukv[l].astype(BF16))
        o_mla = mla_attn(q, kn, vt, krd, batch, seq)

        xt = merge(xt, o_hg, o_rw, o_mla, z_a, w_branch[l].astype(BF16), w_out[l].astype(BF16),
                   _row(mix_post_g[l]))
        xt = ffn(xt, _row(ffn_pre_g[l]), w_ffn_in[l].astype(BF16), w_ffn_out[l].astype(BF16),
                 _row(ffn_post_g[l]))
    return xt.reshape(batch, seq, d)
```

```python
import functools
import itertools
import math

import jax
import jax.numpy as jnp
import numpy as np
from jax import lax
from jax.experimental import pallas as pl
from jax.experimental.pallas import tpu as pltpu

F32 = jnp.float32
BF16 = jnp.bfloat16
LOG2E = math.log2(math.e)

D_MODEL = 1024
NORM_EPS = 1e-6
MASK_VALUE = -1e30
MIN_GATE = 1e-30
HG_HEADS = 8
HG_DK = 128
RW_HEADS = 16
RW_N = 64
RW_WIDTH = RW_HEADS * RW_N
RW_LORA_W = 64
RW_LORA_A = 64
RW_LORA_G = 128
RW_LORA_V = 32
RW_LNX_EPS = 1e-5 * RW_N
MLA_HEADS = 8
MLA_Q_RANK = 384
MLA_KV_RANK = 256
MLA_NOPE = 128
MLA_ROPE = 64
MLA_V = 128
ROPE_THETA = 10000.0
D_FF = 2816

LANES = 128
SUBLANES = 8
VMEM_LIMIT = 48 * 1024 * 1024

PROJ_ROWS = 1024
PROJ_COLS = 3072
RW_PREP_ROWS = 256
MERGE_ROWS = 512
FFN_ROWS = 1024

ZA_HG, ZA_RKV, ZA_GATE, ZA_WIDTH = 0, 3072, 6144, 9216
ZB_F, ZB_CKV, ZB_KR, ZB_WA, ZB_CQ, ZB_G, ZB_VRES = 0, 1024, 1280, 1408, 1536, 1920, 2048


def _cparams(*sem):
    return pltpu.CompilerParams(dimension_semantics=sem, vmem_limit_bytes=VMEM_LIMIT)


def _sigmoid(x):
    return 1.0 / (1.0 + jnp.exp(-x))


def _rms(x, g):
    ms = jnp.mean(x * x, axis=-1, keepdims=True)
    return x * lax.rsqrt(ms + NORM_EPS) * g


def _dot(a, b):
    return jnp.dot(a.astype(BF16), b.astype(BF16), preferred_element_type=F32)


def _dot_nt(a, b):
    return lax.dot_general(a.astype(BF16), b.astype(BF16), (((1,), (1,)), ((), ())), preferred_element_type=F32)


def _dot_tn(a, b):
    return lax.dot_general(a.astype(BF16), b.astype(BF16), (((0,), (0,)), ((), ())), preferred_element_type=F32)


def _tril_ones(n):
    row = lax.broadcasted_iota(jnp.int32, (n, n), 0)
    col = lax.broadcasted_iota(jnp.int32, (n, n), 1)
    return jnp.where(col <= row, 1.0, 0.0).astype(BF16)


def _cumsum_rows(tril, x):
    hi = x.astype(BF16)
    r1 = x - hi.astype(F32)
    mid = r1.astype(BF16)
    lo = (r1 - mid.astype(F32)).astype(BF16)
    return jnp.dot(jnp.concatenate([tril, tril, tril], axis=1), jnp.concatenate([hi, mid, lo], axis=0),
                   preferred_element_type=F32)


def _norm_matmul_kernel(x_ref, g_ref, w_ref, o_ref, h_ref):
    @pl.when(pl.program_id(1) == 0)
    def _():
        h_ref[...] = _rms(x_ref[...], g_ref[...]).astype(BF16)

    o_ref[...] = jnp.dot(h_ref[...], w_ref[...], preferred_element_type=F32).astype(o_ref.dtype)


def norm_matmul(x, g, w, tn, out_dtype):
    T, D = x.shape
    N = w.shape[1]
    tm = PROJ_ROWS
    return pl.pallas_call(
        _norm_matmul_kernel,
        grid=(T // tm, N // tn),
        in_specs=[
            pl.BlockSpec((tm, D), lambda i, j: (i, 0)),
            pl.BlockSpec((1, D), lambda i, j: (0, 0)),
            pl.BlockSpec((D, tn), lambda i, j: (0, j)),
        ],
        out_specs=pl.BlockSpec((tm, tn), lambda i, j: (i, j)),
        out_shape=jax.ShapeDtypeStruct((T, N), out_dtype),
        scratch_shapes=[pltpu.VMEM((tm, D), BF16)],
        compiler_params=_cparams("parallel", "arbitrary"),
        name="norm_matmul",
    )(x, g, w)


HG_CHUNK = 128


def _hgrn_term_codes(C):
    t = np.arange(C)[:, None]
    s = np.arange(C)[None, :]
    codes = np.where(s == t, 0, -1)
    m, i = 1, 0
    while m < C:
        hit = ((t // m) % 2 == 1) & (s // m == t // m - 1)
        codes = np.where(hit, 1 + i, codes)
        m, i = 2 * m, i + 1
    return codes.astype(np.int32)


def _hgrn_stages(q_ref, f_ref, v_ref, og_ref, lb_ref, gn_ref, code_ref, o_ref, st_ref, heads):
    C = q_ref.shape[0]

    def hs(ref, h):
        return ref[:, h * HG_DK:(h + 1) * HG_DK].astype(F32)

    code = code_ref[...]
    tril = _tril_ones(C)
    g8 = C // SUBLANES
    rowc = lax.broadcasted_iota(jnp.int32, (C, HG_DK), 0)
    sub = rowc & (SUBLANES - 1)

    def boundary(bh, m):
        if m >= SUBLANES:
            pieces = [jnp.broadcast_to(bh[r0:r0 + 1, :], (2 * m, HG_DK)) for r0 in range(m - 1, C, 2 * m)]
            return pieces[0] if len(pieces) == 1 else jnp.concatenate(pieces, axis=0)
        if m == 1:
            return jnp.where((sub & 1) == 0, bh, pltpu.roll(bh, shift=1, axis=0))
        b3 = bh.reshape(g8, SUBLANES, HG_DK)

        def tile_row(r):
            return jnp.broadcast_to(b3[:, r:r + 1, :], (g8, SUBLANES, HG_DK)).reshape(C, HG_DK)

        if m == 2:
            return jnp.where(sub < 4, tile_row(1), tile_row(5))
        return tile_row(3)

    q = {h: hs(q_ref, h) for h in heads}
    zf = {h: hs(f_ref, h) for h in heads}
    v = {h: hs(v_ref, h) for h in heads}
    lb = {h: hs(lb_ref, h) for h in heads}
    lf = {h: jnp.log2(jnp.maximum(lb[h] + (1.0 - lb[h]) * _sigmoid(zf[h]), MIN_GATE)) for h in heads}
    k = {h: (1.0 - lb[h]) * _sigmoid(-zf[h]) for h in heads}
    b_all = _cumsum_rows(tril, jnp.concatenate([lf[h] for h in heads], axis=1))
    b = {h: b_all[:, n * HG_DK:(n + 1) * HG_DK] for n, h in enumerate(heads)}
    yield

    scores = {h: jnp.where(code == 0, jnp.sum(q[h] * k[h], axis=-1, keepdims=True), 0.0) for h in heads}
    yield
    m, i = 1, 0
    while m < C:
        odd = ((rowc >> i) & 1) == 1
        sign = jnp.where(odd, 1.0, -1.0)
        hit = code == 1 + i
        xm = {}
        for h in heads:
            d = (b[h] - boundary(b[h], m)) * sign
            xm[h] = (jnp.where(odd, q[h], k[h]) * jnp.exp2(d)).astype(BF16)
        gm = {h: _dot_nt(xm[h], xm[h]) for h in heads}
        scores = {h: jnp.where(hit, gm[h], scores[h]) for h in heads}
        yield
        m, i = 2 * m, i + 1

    st = {h: st_ref[h] for h in heads}
    o = {h: _dot(jnp.concatenate([scores[h].astype(BF16), (q[h] * jnp.exp2(b[h])).astype(BF16)], axis=1),
                 jnp.concatenate([v[h].astype(BF16), st[h].T.astype(BF16)], axis=0)) for h in heads}
    yield
    bl = {h: b[h][C - 1:C, :] for h in heads}
    for h in heads:
        st_ref[h] = st[h] * jnp.exp2(bl[h]) + _dot_tn(v[h], k[h] * jnp.exp2(bl[h] - b[h]))
    yield
    for h in heads:
        og = hs(og_ref, h)
        o_ref[:, h * HG_DK:(h + 1) * HG_DK] = (_rms(o[h], gn_ref[...]) * (og * _sigmoid(og))).astype(o_ref.dtype)


def _head_seg(width):
    r = lax.broadcasted_iota(jnp.int32, (width, width), 0)
    c = lax.broadcasted_iota(jnp.int32, (width, width), 1)
    lg = RW_N.bit_length() - 1
    seg = jnp.where((r >> lg) == (c >> lg), 1.0, 0.0).astype(BF16)
    return jnp.concatenate([seg, seg], axis=0)


def _head_sum(x, seg):
    hi = x.astype(BF16)
    lo = (x - hi.astype(F32)).astype(BF16)
    return jnp.dot(jnp.concatenate([hi, lo], axis=1), seg, preferred_element_type=F32)


PREV_ROWS = 16


def _shift_mix(cur_ref, prev_ref, mu, first):
    z = cur_ref[...].astype(F32)
    last = prev_ref[PREV_ROWS - 1:PREV_ROWS, :].astype(F32)
    last = jnp.where(first, jnp.zeros_like(last), last)
    rolled = pltpu.roll(z, shift=1, axis=0)
    rid = lax.broadcasted_iota(jnp.int32, z.shape, 0)
    zp = jnp.where(rid == 0, jnp.broadcast_to(last, z.shape), rolled)
    return z + (zp - z) * mu


def _rw_prep_kernel(has_vres, seq, *refs):
    if has_vres:
        (r_ref, rp_ref, k_ref, kp_ref, v_ref, vp_ref, wa_ref, wap_ref, g_ref, gp_ref, zv_ref, zvp_ref, vf_ref,
         mu_r, mu_k, mu_v, mu_wa, mu_g, mu_zv, w0, w_up, a0, a_up, g_up, v0, v_up, k_k, k_a,
         ro, ldo, ko, vo, kko, ao, go) = refs
    else:
        (r_ref, rp_ref, k_ref, kp_ref, v_ref, vp_ref, wa_ref, wap_ref, g_ref, gp_ref,
         mu_r, mu_k, mu_v, mu_wa, mu_g, w0, w_up, a0, a_up, g_up, k_k, k_a,
         ro, ldo, ko, vo, kko, ao, go) = refs
    tm = r_ref.shape[0]
    first = (pl.program_id(0) * tm) % seq == 0

    r = _shift_mix(r_ref, rp_ref, mu_r[...], first)
    k = _shift_mix(k_ref, kp_ref, mu_k[...], first)
    v = _shift_mix(v_ref, vp_ref, mu_v[...], first)
    wa = _shift_mix(wa_ref, wap_ref, mu_wa[...], first)
    zg = _shift_mix(g_ref, gp_ref, mu_g[...], first)

    zw = w0[...] + _dot(jnp.tanh(wa), w_up[...])
    w_log = -(jnp.maximum(-zw, 0.0) + jnp.log(1.0 + jnp.exp(-jnp.abs(zw)))) - 0.5
    ldo[...] = -jnp.exp(w_log)
    a = _sigmoid(a0[...] + _dot(wa, a_up[...]))
    go[...] = _dot(_sigmoid(zg), g_up[...]).astype(go.dtype)
    if has_vres:
        zv = _shift_mix(zv_ref, zvp_ref, mu_zv[...], first)
        v = v + (vf_ref[...].astype(F32) - v) * _sigmoid(v0[...] + _dot(zv, v_up[...]))
    kk = k * k_k[...]
    seg = _head_seg(LANES)
    for p in range(RW_WIDTH // LANES):
        sl = slice(p * LANES, (p + 1) * LANES)
        kkp = kk[:, sl]
        ss = _head_sum(kkp * kkp, seg)
        kko[:, sl] = (kkp * lax.rsqrt(jnp.maximum(ss, 1e-24))).astype(kko.dtype)
    ro[...] = r.astype(ro.dtype)
    ko[...] = (k * (1.0 + (a - 1.0) * k_a[...])).astype(ko.dtype)
    vo[...] = v.astype(vo.dtype)
    ao[...] = a.astype(ao.dtype)


def rw_prep(z_a, z_b, v_first, p, seq):
    T = z_a.shape[0]
    has_vres = v_first is not None
    W = RW_WIDTH
    tm = RW_PREP_ROWS
    tb = tm // PREV_ROWS

    def cur(width, off):
        return pl.BlockSpec((tm, width), lambda i: (i, off // width))

    def prev(width, off):
        return pl.BlockSpec((PREV_ROWS, width), lambda i: (jnp.maximum(i * tb - 1, 0), off // width))

    def full(a):
        return pl.BlockSpec(a.shape, lambda i: (0,) * a.ndim)

    acts = [z_a, z_a, z_a, z_a, z_a, z_a, z_b, z_b, z_b, z_b]
    specs = [cur(W, ZA_RKV), prev(W, ZA_RKV), cur(W, ZA_RKV + W), prev(W, ZA_RKV + W),
             cur(W, ZA_RKV + 2 * W), prev(W, ZA_RKV + 2 * W),
             cur(LANES, ZB_WA), prev(LANES, ZB_WA), cur(LANES, ZB_G), prev(LANES, ZB_G)]
    if has_vres:
        acts += [z_b, z_b, v_first]
        specs += [cur(LANES, ZB_VRES), prev(LANES, ZB_VRES), pl.BlockSpec((tm, W), lambda i: (i, 0))]
        names = ["mu_r", "mu_k", "mu_v", "mu_wa", "mu_g", "mu_zv", "w0", "w_up", "a0", "a_up", "g_up", "v0", "v_up",
                 "k_k", "k_a"]
    else:
        names = ["mu_r", "mu_k", "mu_v", "mu_wa", "mu_g", "w0", "w_up", "a0", "a_up", "g_up", "k_k", "k_a"]
    params = [p[n] for n in names]
    out_spec = pl.BlockSpec((tm, W), lambda i: (i, 0))
    return pl.pallas_call(
        functools.partial(_rw_prep_kernel, has_vres, seq),
        grid=(T // tm,),
        in_specs=specs + [full(a) for a in params],
        out_specs=[out_spec] * 7,
        out_shape=[jax.ShapeDtypeStruct((T, W), F32 if i == 1 else BF16) for i in range(7)],
        compiler_params=_cparams("parallel"),
        name="rw_prep",
    )(*acts, *params)


HG_HEADS_PER_PASS = 4
RW_STAGES_PER_STEP = 1
HG_STAGES_PER_STEP = 1
RW_CHUNK = 64


def _rw_chunk_stages(r_ref, ld_ref, k_ref, v_ref, kk_ref, a_ref, g_ref, rk_ref, lg_ref, lb_ref, o_ref, st_ref, row0,
                     pairs):
    C = RW_CHUNK
    C2 = 2 * C
    rows = slice(row0, row0 + C)

    def ps(ref, p):
        return ref[rows if ref.shape[0] > 1 else slice(None), p * LANES:(p + 1) * LANES].astype(F32)

    tril = _tril_ones(C)
    seg = _head_seg(LANES)
    lane = lax.broadcasted_iota(jnp.int32, (C, LANES), 1)
    row = lax.broadcasted_iota(jnp.int32, (C2, C2), 0)
    col = lax.broadcasted_iota(jnp.int32, (C2, C2), 1)
    strict = (col & (C - 1)) < (row & (C - 1))
    incl = (col & (C - 1)) <= (row & (C - 1))
    eye = jnp.where(row == col, 1.0, 0.0)

    def stack(x):
        return jnp.concatenate([jnp.where(lane < RW_N, x, 0.0), jnp.where(lane >= RW_N, x, 0.0)], axis=0).astype(BF16)

    r = {p: ps(r_ref, p) for p in pairs}
    ld = {p: ps(ld_ref, p) for p in pairs}
    k = {p: ps(k_ref, p) for p in pairs}
    v = {p: ps(v_ref, p) for p in pairs}
    kk = {p: ps(kk_ref, p) for p in pairs}
    kka = {p: kk[p] * ps(a_ref, p) for p in pairs}
    cl_all = _cumsum_rows(tril, ld_ref[rows, pairs[0] * LANES:(pairs[-1] + 1) * LANES])
    cl = {p: cl_all[:, n * LANES:(n + 1) * LANES] for n, p in enumerate(pairs)}
    gc = {p: cl[p][C - 1:C, :] for p in pairs}
    e_inv = {p: jnp.exp(-cl[p]) for p in pairs}
    e_end = {p: jnp.exp(gc[p] - cl[p]) for p in pairs}
    ar = {p: jnp.concatenate([stack(-kk[p] * jnp.exp(cl[p] - ld[p])), stack(r[p] * jnp.exp(cl[p]))], axis=0)
          for p in pairs}
    bk = {p: jnp.concatenate([stack(kka[p] * e_inv[p]), stack(k[p] * e_inv[p])], axis=0) for p in pairs}
    bkg = {p: jnp.concatenate([stack(kka[p] * e_end[p]), stack(k[p] * e_end[p])], axis=0) for p in pairs}
    vv = {p: stack(v[p]) for p in pairs}
    yield

    g4 = {p: _dot_nt(ar[p], bk[p]) for p in pairs}
    lab = {p: jnp.where(strict, g4[p][:C2, :C2], 0.0) for p in pairs}
    lkm = {p: jnp.concatenate([jnp.where(strict, g4[p][:C2, C2:], 0.0), jnp.where(incl, g4[p][C2:, C2:], 0.0)],
                              axis=0).astype(BF16) for p in pairs}
    mb = {p: jnp.where(incl, g4[p][C2:, :C2], 0.0).astype(BF16) for p in pairs}
    yield
    qi = {p: eye + lab[p] for p in pairs}
    mp = {p: _dot(lab[p], lab[p]) for p in pairs}
    yield
    n = 2
    while 2 * n < C:
        mq = {p: _dot(mp[p], jnp.concatenate([mp[p].astype(BF16), qi[p].astype(BF16)], axis=1)) for p in pairs}
        mp = {p: mq[p][:, :C2] for p in pairs}
        qi = {p: qi[p] + mq[p][:, C2:] for p in pairs}
        yield
        n *= 2
    qi = {p: qi[p] + _dot(mp[p], qi[p]) for p in pairs}
    yield

    zt = {p: st_ref[p] for p in pairs}
    fz = {p: _dot(jnp.concatenate([ar[p], lkm[p]], axis=1), jnp.concatenate([zt[p].T.astype(BF16), vv[p]], axis=0))
          for p in pairs}
    yield
    uu = {p: _dot(qi[p], fz[p][:C2]) for p in pairs}
    yield
    yy = {p: fz[p][C2:] + _dot(mb[p], uu[p]) for p in pairs}
    for p in pairs:
        uv = jnp.concatenate([uu[p].astype(BF16), vv[p]], axis=0)
        st_ref[p] = zt[p] * jnp.exp(gc[p]) + _dot_tn(uv, bkg[p])
    yield

    y = {p: yy[p][:C] + yy[p][C:] for p in pairs}
    mean = {p: _head_sum(y[p], seg) * (1.0 / RW_N) for p in pairs}
    d = {p: y[p] - mean[p] for p in pairs}
    var = {p: _head_sum(d[p] * d[p], seg) * (1.0 / RW_N) for p in pairs}
    bonus = {p: _head_sum(r[p] * k[p] * ps(rk_ref, p), seg) * v[p] for p in pairs}
    yield
    for p in pairs:
        yn = d[p] * lax.rsqrt(var[p] + RW_LNX_EPS) * ps(lg_ref, p) + ps(lb_ref, p)
        o_ref[rows, p * LANES:(p + 1) * LANES] = ((yn + bonus[p]) * ps(g_ref, p)).astype(o_ref.dtype)


def _advance(gen, n):
    end = object()
    while True:
        for _ in range(n):
            if next(gen, end) is end:
                return
        yield


def _recurrent_kernel(hq, hf, hv, hog, hlb, hgn, hcode, r, ld, k, v, kk, a, g, rk, lg, lb, o_hg, o_rw, hst, rst):
    @pl.when(pl.program_id(1) == 0)
    def _():
        hst[...] = jnp.zeros_like(hst)
        rst[...] = jnp.zeros_like(rst)

    rw_refs = (r, ld, k, v, kk, a, g, rk, lg, lb, o_rw, rst)
    pairs = range(r.shape[1] // LANES)
    rw = itertools.chain(*[_rw_chunk_stages(*rw_refs, row0, pairs) for row0 in range(0, HG_CHUNK, RW_CHUNK)])
    n = HG_HEADS_PER_PASS
    hg = itertools.chain(*[_hgrn_stages(hq, hf, hv, hog, hlb, hgn, hcode, o_hg, hst, range(h0, h0 + n))
                           for h0 in range(0, HG_HEADS, n)])
    for _ in itertools.zip_longest(_advance(rw, RW_STAGES_PER_STEP), _advance(hg, HG_STAGES_PER_STEP)):
        pass


def recurrent_mixers(z_a, z_b, lower_bound, onorm_g, r, ld, k, v, kk, a, g, r_k, lnx_g, lnx_b, batch, seq):
    T, W = r.shape
    C = HG_CHUNK
    nc = seq // C
    codes = jnp.asarray(_hgrn_term_codes(C))

    def zspec(seg):
        return pl.BlockSpec((C, W), lambda b, c, seg=seg: (b * nc + c, ZA_HG // W + seg))

    act = pl.BlockSpec((C, W), lambda b, c: (b * nc + c, 0))
    par = pl.BlockSpec((1, W), lambda b, c: (0, 0))
    return pl.pallas_call(
        _recurrent_kernel,
        grid=(batch, nc),
        in_specs=[
            zspec(0), pl.BlockSpec((C, W), lambda b, c: (b * nc + c, ZB_F // W)), zspec(1), zspec(2),
            par,
            pl.BlockSpec((1, HG_DK), lambda b, c: (0, 0)),
            pl.BlockSpec((C, C), lambda b, c: (0, 0)),
        ] + [act] * 7 + [par] * 3,
        out_specs=[act, act],
        out_shape=[jax.ShapeDtypeStruct((T, W), BF16)] * 2,
        scratch_shapes=[pltpu.VMEM((HG_HEADS, HG_DK, HG_DK), F32), pltpu.VMEM((W // LANES, LANES, LANES), F32)],
        compiler_params=_cparams("parallel", "arbitrary"),
        name="recurrent_mixers",
    )(z_a, z_b, z_a, z_a, lower_bound, onorm_g, codes, r, ld, k, v, kk, a, g, r_k, lnx_g, lnx_b)


MLA_QK = 2 * LANES
MLA_TQ = 512
MLA_HEADS_PER_STEP = 4
MLA_VT_ROWS = MLA_V + 16


def _mla_up_kernel(cq_ref, ckv_ref, kr_ref, cs_ref, gq_ref, gkv_ref, wq_ref, wkv_ref, q_ref, kn_ref, vt_ref, krd_ref):
    scale = (MLA_NOPE + MLA_ROPE) ** -0.5 * LOG2E
    cs = cs_ref[...]
    q = _dot(_rms(cq_ref[...], gq_ref[...]), wq_ref[...])
    for h in range(MLA_HEADS):
        o = h * MLA_QK
        q_ref[:, o:o + LANES] = (q[:, o:o + LANES] * scale).astype(q_ref.dtype)
        q_ref[:, o + LANES:o + 2 * LANES] = (q[:, o + LANES:o + 2 * LANES] * (cs * scale)).astype(q_ref.dtype)
    kv = _dot(_rms(ckv_ref[...], gkv_ref[...]), wkv_ref[...])
    for h in range(MLA_HEADS):
        o = h * (MLA_NOPE + MLA_V)
        kn_ref[:, h * MLA_NOPE:(h + 1) * MLA_NOPE] = kv[:, o:o + MLA_NOPE].astype(kn_ref.dtype)
        vt_ref[h, 0, :MLA_V, :] = kv[:, o + MLA_NOPE:o + MLA_NOPE + MLA_V].T.astype(vt_ref.dtype)
        vt_ref[h, 0, MLA_V:, :] = jnp.ones((MLA_VT_ROWS - MLA_V, kv.shape[0]), vt_ref.dtype)
    krx = kr_ref[...] * cs
    krd_ref[...] = (krx + pltpu.roll(krx, shift=MLA_ROPE, axis=1)).astype(krd_ref.dtype)


def mla_up(z_b, cs, gq, gkv, wq, wkv):
    T = z_b.shape[0]
    tm = MLA_TQ
    nq = wq.shape[1]

    def full(a):
        return pl.BlockSpec(a.shape, lambda i: (0,) * a.ndim)

    return pl.pallas_call(
        _mla_up_kernel,
        grid=(T // tm,),
        in_specs=[
            pl.BlockSpec((tm, MLA_Q_RANK), lambda i: (i, ZB_CQ // MLA_Q_RANK)),
            pl.BlockSpec((tm, MLA_KV_RANK), lambda i: (i, ZB_CKV // MLA_KV_RANK)),
            pl.BlockSpec((tm, LANES), lambda i: (i, ZB_KR // LANES)),
            pl.BlockSpec((tm, LANES), lambda i: (i, 0)),
            full(gq), full(gkv), full(wq), full(wkv),
        ],
        out_specs=[
            pl.BlockSpec((tm, nq), lambda i: (i, 0)),
            pl.BlockSpec((tm, MLA_HEADS * MLA_NOPE), lambda i: (i, 0)),
            pl.BlockSpec((MLA_HEADS, 1, MLA_VT_ROWS, tm), lambda i: (0, i, 0, 0)),
            pl.BlockSpec((tm, LANES), lambda i: (i, 0)),
        ],
        out_shape=[
            jax.ShapeDtypeStruct((T, nq), BF16),
            jax.ShapeDtypeStruct((T, MLA_HEADS * MLA_NOPE), BF16),
            jax.ShapeDtypeStruct((MLA_HEADS, T // tm, MLA_VT_ROWS, tm), BF16),
            jax.ShapeDtypeStruct((T, LANES), BF16),
        ],
        compiler_params=_cparams("parallel"),
        name="mla_up",
    )(z_b, z_b, z_b, cs, gq, gkv, wq, wkv)


def _mla_attn_kernel(q_ref, kn_ref, kr_ref, vt_ref, o_ref, m_ref, acc_ref, s_ref):
    tq = q_ref.shape[0]
    heads = range(q_ref.shape[1] // MLA_QK)
    qi = pl.program_id(2)
    q = [q_ref[:, h * MLA_QK:(h + 1) * MLA_QK] for h in heads]
    m_ref[...] = jnp.full_like(m_ref, -jnp.inf)
    acc_ref[...] = jnp.zeros_like(acc_ref)
    row = lax.broadcasted_iota(jnp.int32, (tq, tq), 0)
    col = lax.broadcasted_iota(jnp.int32, (tq, tq), 1)
    causal = row <= col

    def scores(kb):
        ks = pl.multiple_of(kb * tq, tq)
        kr = kr_ref[pl.ds(ks, tq), :]
        kc = [jnp.concatenate([kn_ref[pl.ds(ks, tq), h * MLA_NOPE:(h + 1) * MLA_NOPE], kr], axis=1) for h in heads]
        return [lax.dot_general(kc[h], q[h], (((1,), (1,)), ((), ())), preferred_element_type=F32)
                for h in heads]

    def update(kb):
        s = [s_ref[h] for h in heads]
        m_old = [m_ref[h] for h in heads]
        m_new = [jnp.maximum(m_old[h], jnp.max(s[h], axis=0, keepdims=True)) for h in heads]
        p = [jnp.exp2(s[h] - m_new[h]).astype(BF16) for h in heads]
        for h in heads:
            acc_ref[h] = (jnp.exp2(m_old[h] - m_new[h]) * acc_ref[h]
                          + jnp.dot(vt_ref[h, kb], p[h], preferred_element_type=F32))
            m_ref[h] = m_new[h]

    s0 = scores(0)
    for h in heads:
        s_ref[h] = jnp.where(qi > 0, s0[h], jnp.where(causal, s0[h], MASK_VALUE))

    def body(kb, carry):
        s_next = scores(kb + 1)
        update(kb)
        for h in heads:
            s_ref[h] = s_next[h]
        return carry

    lax.fori_loop(0, qi - 1, body, 0)

    @pl.when(qi > 0)
    def _():
        s_next = scores(qi)
        update(qi - 1)
        for h in heads:
            s_ref[h] = jnp.where(causal, s_next[h], MASK_VALUE)

    update(qi)
    for h in heads:
        acc = acc_ref[h]
        o_ref[:, h * MLA_V:(h + 1) * MLA_V] = (acc[:MLA_V] / acc[MLA_V:MLA_V + 1]).T.astype(o_ref.dtype)


def mla_attn(q, kn, vt, krd, batch, seq):
    T = q.shape[0]
    tq = MLA_TQ
    nq = seq // tq
    g = MLA_HEADS_PER_STEP
    return pl.pallas_call(
        _mla_attn_kernel,
        grid=(batch, MLA_HEADS // g, nq),
        in_specs=[
            pl.BlockSpec((tq, g * MLA_QK), lambda b, h, i: (b * nq + i, h)),
            pl.BlockSpec((seq, g * MLA_NOPE), lambda b, h, i: (b, h)),
            pl.BlockSpec((seq, LANES), lambda b, h, i: (b, 0)),
            pl.BlockSpec((g, nq, MLA_VT_ROWS, tq), lambda b, h, i: (h, b, 0, 0)),
        ],
        out_specs=pl.BlockSpec((tq, g * MLA_V), lambda b, h, i: (b * nq + i, h)),
        out_shape=jax.ShapeDtypeStruct((T, MLA_HEADS * MLA_V), BF16),
        scratch_shapes=[pltpu.VMEM((g, 1, tq), F32), pltpu.VMEM((g, MLA_VT_ROWS, tq), F32),
                        pltpu.VMEM((g, tq, tq), F32)],
        compiler_params=_cparams("parallel", "parallel", "arbitrary"),
        name="mla_attn",
    )(q, kn, krd, vt)


def _merge_kernel(x_ref, ohg_ref, orw_ref, omla_ref, g0_ref, g1_ref, g2_ref, wb_ref, wo_ref, gp_ref, o_ref):
    merged = (_sigmoid(g0_ref[...].astype(F32)) * jnp.dot(ohg_ref[...], wb_ref[0], preferred_element_type=F32)
              + _sigmoid(g1_ref[...].astype(F32)) * jnp.dot(orw_ref[...], wb_ref[1], preferred_element_type=F32)
              + _sigmoid(g2_ref[...].astype(F32)) * jnp.dot(omla_ref[...], wb_ref[2], preferred_element_type=F32))
    y = _dot(merged, wo_ref[...])
    o_ref[...] = x_ref[...] + _rms(y, gp_ref[...])


def merge(x, o_hg, o_rw, o_mla, z_a, w_branch, w_out, g_post):
    T, D = x.shape
    tm = MERGE_ROWS
    row = pl.BlockSpec((tm, D), lambda i: (i, 0))

    def gate(n):
        return pl.BlockSpec((tm, D), lambda i, n=n: (i, ZA_GATE // D + n))

    def full(a):
        return pl.BlockSpec(a.shape, lambda i: (0,) * a.ndim)

    return pl.pallas_call(
        _merge_kernel,
        grid=(T // tm,),
        in_specs=[row, row, row, row, gate(0), gate(1), gate(2), full(w_branch), full(w_out), full(g_post)],
        out_specs=row,
        out_shape=jax.ShapeDtypeStruct((T, D), F32),
        compiler_params=_cparams("parallel"),
        name="merge",
    )(x, o_hg, o_rw, o_mla, z_a, z_a, z_a, w_branch, w_out, g_post)


FFN_TILE = 256


def _ffn_kernel(x_ref, gpre_ref, wg_ref, wu_ref, wo_ref, gpost_ref, o_ref, h_ref, acc_ref):
    j = pl.program_id(1)

    @pl.when(j == 0)
    def _():
        h_ref[...] = _rms(x_ref[...], gpre_ref[...]).astype(BF16)
        acc_ref[...] = jnp.zeros_like(acc_ref)

    h = h_ref[...]
    gate = jnp.dot(h, wg_ref[...], preferred_element_type=F32)
    up = jnp.dot(h, wu_ref[...], preferred_element_type=F32)
    act = gate * _sigmoid(gate) * up
    acc_ref[...] += _dot(act, wo_ref[...])

    @pl.when(j == pl.num_programs(1) - 1)
    def _():
        o_ref[...] = x_ref[...] + _rms(acc_ref[...], gpost_ref[...])


def ffn(x, g_pre, w_in, w_out, g_post):
    T, D = x.shape
    tm = FFN_ROWS
    tf = FFN_TILE
    nf = D_FF // tf
    return pl.pallas_call(
        _ffn_kernel,
        grid=(T // tm, nf),
        in_specs=[
            pl.BlockSpec((tm, D), lambda i, j: (i, 0)),
            pl.BlockSpec((1, D), lambda i, j: (0, 0)),
            pl.BlockSpec((D, tf), lambda i, j: (0, j)),
            pl.BlockSpec((D, tf), lambda i, j: (0, nf + j)),
            pl.BlockSpec((tf, D), lambda i, j: (j, 0)),
            pl.BlockSpec((1, D), lambda i, j: (0, 0)),
        ],
        out_specs=pl.BlockSpec((tm, D), lambda i, j: (i, 0)),
        out_shape=jax.ShapeDtypeStruct((T, D), F32),
        scratch_shapes=[pltpu.VMEM((tm, D), BF16), pltpu.VMEM((tm, D), F32)],
        compiler_params=_cparams("parallel", "arbitrary"),
        name="ffn",
    )(x, g_pre, w_in, w_in, w_out, g_post)


def _swap_halves(w):
    h = w.shape[-1] // 2
    return jnp.concatenate([w[..., h:], w[..., :h]], axis=-1)


def _pad_rows(w, rows, at):
    out = jnp.zeros((rows, w.shape[1]), w.dtype)
    return out.at[at:at + w.shape[0]].set(w)


def _pack_in_proj(w, w_vres):
    hg, rkv, wa, g, cq, ckv, kr, gates = jnp.split(
        w, np.cumsum([4096, 3072, RW_LORA_W + RW_LORA_A, RW_LORA_G, MLA_Q_RANK, MLA_KV_RANK, MLA_ROPE]).tolist(),
        axis=1)
    hq, hf, hi, hog = jnp.split(hg, 4, axis=1)
    wide = jnp.concatenate([hq, hi, hog, rkv, gates], axis=1)
    parts = [hf, ckv, kr, _swap_halves(kr), wa, cq, g]
    if w_vres is not None:
        parts.append(jnp.pad(w_vres, ((0, 0), (0, LANES - RW_LORA_V))))
    narrow = jnp.concatenate(parts, axis=1)
    return wide.astype(BF16), narrow.astype(BF16)


def _pack_wq(w_uq):
    w = w_uq.reshape(MLA_Q_RANK, MLA_HEADS, MLA_NOPE + MLA_ROPE)
    rope = w[..., MLA_NOPE:]
    w = jnp.concatenate([w[..., :MLA_NOPE], rope, _swap_halves(rope)], axis=-1)
    return w.reshape(MLA_Q_RANK, MLA_HEADS * MLA_QK).astype(BF16)


def _row(v):
    return v.reshape(1, -1).astype(F32)


def kernel(x, positions, hgrn_lb_logits, mix_pre_g, mix_post_g, ffn_pre_g, ffn_post_g, w_in, w_vres_down,
           hgrn_onorm_g, rwkv_mu, rwkv_vres_mu, rwkv_w0, rwkv_w_up, rwkv_a0, rwkv_a_up, rwkv_g_up, rwkv_v0,
           rwkv_v_up, rwkv_k_k, rwkv_k_a, rwkv_r_k, rwkv_lnx_g, rwkv_lnx_b, mla_q_norm_g, mla_w_uq,
           mla_kv_norm_g, mla_w_ukv, w_branch, w_out, w_ffn_in, w_ffn_out):
    batch, seq, d = x.shape
    depth = w_in.shape[0]
    T = batch * seq
    xt = x.reshape(T, d)

    probs = jax.nn.softmax(hgrn_lb_logits.astype(F32), axis=0)
    lower_bounds = jnp.cumsum(probs, axis=0) - probs[0]
    inv_freq = ROPE_THETA ** (-jnp.arange(0, MLA_ROPE, 2, dtype=F32) / MLA_ROPE)
    ang = positions.astype(F32).reshape(T, 1) * inv_freq
    cos, sin = jnp.cos(ang), jnp.sin(ang)
    cs = jnp.concatenate([cos, cos, -sin, sin], axis=-1)

    W = RW_WIDTH
    v_first = None
    for l in range(depth):
        w_wide, w_narrow = _pack_in_proj(w_in[l], None if l == 0 else w_vres_down[l - 1])
        g_pre = _row(mix_pre_g[l])
        z_a = norm_matmul(xt, g_pre, w_wide, PROJ_COLS, BF16)
        z_b = norm_matmul(xt, g_pre, w_narrow, w_narrow.shape[1], F32)

        mu = rwkv_mu[l]
        p = {
            "mu_r": _row(mu[:W]), "mu_k": _row(mu[W:2 * W]), "mu_v": _row(mu[2 * W:3 * W]),
            "mu_wa": _row(mu[3 * W:3 * W + LANES]), "mu_g": _row(mu[3 * W + LANES:]),
            "w0": _row(rwkv_w0[l]), "w_up": _pad_rows(rwkv_w_up[l], LANES, 0).astype(BF16),
            "a0": _row(rwkv_a0[l]), "a_up": _pad_rows(rwkv_a_up[l], LANES, RW_LORA_W).astype(BF16),
            "g_up": rwkv_g_up[l].astype(BF16),
            "k_k": _row(rwkv_k_k[l]), "k_a": _row(rwkv_k_a[l]),
        }
        if l > 0:
            p["mu_zv"] = _row(jnp.pad(rwkv_vres_mu[l - 1], (0, LANES - RW_LORA_V)))
            p["v0"] = _row(rwkv_v0[l - 1])
            p["v_up"] = _pad_rows(rwkv_v_up[l - 1], LANES, 0).astype(BF16)
        r, ld, k, v, kk, a, g = rw_prep(z_a, z_b, v_first, p, seq)
        if l == 0:
            v_first = v
        o_hg, o_rw = recurrent_mixers(z_a, z_b, _row(lower_bounds[l]), _row(hgrn_onorm_g[l]), r, ld, k, v, kk, a, g,
                                      _row(rwkv_r_k[l]), _row(rwkv_lnx_g[l]), _row(rwkv_lnx_b[l]), batch, seq)

        q, kn, vt, krd = mla_up(z_b, cs, _row(mla_q_norm_g[l]), _row(mla_kv_norm_g[l]), _pack_wq(mla_w_uq[l]),
                                mla_w_ukv[l].astype(BF16))
        o_mla = mla_attn(q, kn, vt, krd, batch, seq)

        xt = merge(xt, o_hg, o_rw, o_mla, z_a, w_branch[l].astype(BF16), w_out[l].astype(BF16),
                   _row(mix_post_g[l]))
        xt = ffn(xt, _row(ffn_pre_g[l]), w_ffn_in[l].astype(BF16), w_ffn_out[l].astype(BF16),
                 _row(ffn_post_g[l]))
    return xt.reshape(batch, seq, d)
```

```python
import functools
import itertools
import math

import jax
import jax.numpy as jnp
import numpy as np
from jax import lax
from jax.experimental import pallas as pl
from jax.experimental.pallas import tpu as pltpu

F32 = jnp.float32
BF16 = jnp.bfloat16
LOG2E = math.log2(math.e)

D_MODEL = 1024
NORM_EPS = 1e-6
MASK_VALUE = -1e30
MIN_GATE = 1e-30
HG_HEADS = 8
HG_DK = 128
RW_HEADS = 16
RW_N = 64
RW_WIDTH = RW_HEADS * RW_N
RW_LORA_W = 64
RW_LORA_A = 64
RW_LORA_G = 128
RW_LORA_V = 32
RW_LNX_EPS = 1e-5 * RW_N
MLA_HEADS = 8
MLA_Q_RANK = 384
MLA_KV_RANK = 256
MLA_NOPE = 128
MLA_ROPE = 64
MLA_V = 128
ROPE_THETA = 10000.0
D_FF = 2816

LANES = 128
SUBLANES = 8
VMEM_LIMIT = 48 * 1024 * 1024

PROJ_ROWS = 1024
PROJ_COLS = 3072
RW_PREP_ROWS = 256
MERGE_ROWS = 512
FFN_ROWS = 1024

ZA_HG, ZA_RKV, ZA_GATE, ZA_WIDTH = 0, 3072, 6144, 9216
ZB_F, ZB_CKV, ZB_KR, ZB_WA, ZB_CQ, ZB_G, ZB_VRES = 0, 1024, 1280, 1408, 1536, 1920, 2048


def _cparams(*sem):
    return pltpu.CompilerParams(dimension_semantics=sem, vmem_limit_bytes=VMEM_LIMIT)


def _sigmoid(x):
    return 1.0 / (1.0 + jnp.exp(-x))


def _rms(x, g):
    ms = jnp.mean(x * x, axis=-1, keepdims=True)
    return x * lax.rsqrt(ms + NORM_EPS) * g


def _dot(a, b):
    return jnp.dot(a.astype(BF16), b.astype(BF16), preferred_element_type=F32)


def _dot_nt(a, b):
    return lax.dot_general(a.astype(BF16), b.astype(BF16), (((1,), (1,)), ((), ())), preferred_element_type=F32)


def _dot_tn(a, b):
    return lax.dot_general(a.astype(BF16), b.astype(BF16), (((0,), (0,)), ((), ())), preferred_element_type=F32)


def _tril_ones(n):
    row = lax.broadcasted_iota(jnp.int32, (n, n), 0)
    col = lax.broadcasted_iota(jnp.int32, (n, n), 1)
    return jnp.where(col <= row, 1.0, 0.0).astype(BF16)


def _cumsum_rows(tril, x):
    hi = x.astype(BF16)
    r1 = x - hi.astype(F32)
    mid = r1.astype(BF16)
    lo = (r1 - mid.astype(F32)).astype(BF16)
    return jnp.dot(jnp.concatenate([tril, tril, tril], axis=1), jnp.concatenate([hi, mid, lo], axis=0),
                   preferred_element_type=F32)


def _norm_matmul_kernel(x_ref, g_ref, w_ref, o_ref, h_ref):
    @pl.when(pl.program_id(1) == 0)
    def _():
        h_ref[...] = _rms(x_ref[...], g_ref[...]).astype(BF16)

    o_ref[...] = jnp.dot(h_ref[...], w_ref[...], preferred_element_type=F32).astype(o_ref.dtype)


def norm_matmul(x, g, w, tn, out_dtype):
    T, D = x.shape
    N = w.shape[1]
    tm = PROJ_ROWS
    return pl.pallas_call(
        _norm_matmul_kernel,
        grid=(T // tm, N // tn),
        in_specs=[
            pl.BlockSpec((tm, D), lambda i, j: (i, 0)),
            pl.BlockSpec((1, D), lambda i, j: (0, 0)),
            pl.BlockSpec((D, tn), lambda i, j: (0, j)),
        ],
        out_specs=pl.BlockSpec((tm, tn), lambda i, j: (i, j)),
        out_shape=jax.ShapeDtypeStruct((T, N), out_dtype),
        scratch_shapes=[pltpu.VMEM((tm, D), BF16)],
        compiler_params=_cparams("parallel", "arbitrary"),
        name="norm_matmul",
    )(x, g, w)


HG_CHUNK = 128


def _hgrn_term_codes(C):
    t = np.arange(C)[:, None]
    s = np.arange(C)[None, :]
    codes = np.where(s == t, 0, -1)
    m, i = 1, 0
    while m < C:
        hit = ((t // m) % 2 == 1) & (s // m == t // m - 1)
        codes = np.where(hit, 1 + i, codes)
        m, i = 2 * m, i + 1
    return codes.astype(np.int32)


def _hgrn_stages(q_ref, f_ref, v_ref, og_ref, lb_ref, gn_ref, code_ref, o_ref, st_ref, heads):
    C = q_ref.shape[0]

    def hs(ref, h):
        return ref[:, h * HG_DK:(h + 1) * HG_DK].astype(F32)

    code = code_ref[...]
    tril = _tril_ones(C)
    g8 = C // SUBLANES
    rowc = lax.broadcasted_iota(jnp.int32, (C, HG_DK), 0)
    sub = rowc & (SUBLANES - 1)

    def boundary(bh, m):
        if m >= SUBLANES:
            pieces = [jnp.broadcast_to(bh[r0:r0 + 1, :], (2 * m, HG_DK)) for r0 in range(m - 1, C, 2 * m)]
            return pieces[0] if len(pieces) == 1 else jnp.concatenate(pieces, axis=0)
        if m == 1:
            return jnp.where((sub & 1) == 0, bh, pltpu.roll(bh, shift=1, axis=0))
        b3 = bh.reshape(g8, SUBLANES, HG_DK)

        def tile_row(r):
            return jnp.broadcast_to(b3[:, r:r + 1, :], (g8, SUBLANES, HG_DK)).reshape(C, HG_DK)

        if m == 2:
            return jnp.where(sub < 4, tile_row(1), tile_row(5))
        return tile_row(3)

    q = {h: hs(q_ref, h) for h in heads}
    zf = {h: hs(f_ref, h) for h in heads}
    v = {h: hs(v_ref, h) for h in heads}
    lb = {h: hs(lb_ref, h) for h in heads}
    lf = {h: jnp.log2(jnp.maximum(lb[h] + (1.0 - lb[h]) * _sigmoid(zf[h]), MIN_GATE)) for h in heads}
    k = {h: (1.0 - lb[h]) * _sigmoid(-zf[h]) for h in heads}
    b_all = _cumsum_rows(tril, jnp.concatenate([lf[h] for h in heads], axis=1))
    b = {h: b_all[:, n * HG_DK:(n + 1) * HG_DK] for n, h in enumerate(heads)}
    yield

    scores = {h: jnp.where(code == 0, jnp.sum(q[h] * k[h], axis=-1, keepdims=True), 0.0) for h in heads}
    yield
    m, i = 1, 0
    while m < C:
        odd = ((rowc >> i) & 1) == 1
        sign = jnp.where(odd, 1.0, -1.0)
        hit = code == 1 + i
        xm = {}
        for h in heads:
            d = (b[h] - boundary(b[h], m)) * sign
            xm[h] = (jnp.where(odd, q[h], k[h]) * jnp.exp2(d)).astype(BF16)
        gm = {h: _dot_nt(xm[h], xm[h]) for h in heads}
        scores = {h: jnp.where(hit, gm[h], scores[h]) for h in heads}
        yield
        m, i = 2 * m, i + 1

    st = {h: st_ref[h] for h in heads}
    o = {h: _dot(jnp.concatenate([scores[h].astype(BF16), (q[h] * jnp.exp2(b[h])).astype(BF16)], axis=1),
                 jnp.concatenate([v[h].astype(BF16), st[h].T.astype(BF16)], axis=0)) for h in heads}
    yield
    bl = {h: b[h][C - 1:C, :] for h in heads}
    for h in heads:
        st_ref[h] = st[h] * jnp.exp2(bl[h]) + _dot_tn(v[h], k[h] * jnp.exp2(bl[h] - b[h]))
    yield
    for h in heads:
        og = hs(og_ref, h)
        o_ref[:, h * HG_DK:(h + 1) * HG_DK] = (_rms(o[h], gn_ref[...]) * (og * _sigmoid(og))).astype(o_ref.dtype)


def _head_seg(width):
    r = lax.broadcasted_iota(jnp.int32, (width, width), 0)
    c = lax.broadcasted_iota(jnp.int32, (width, width), 1)
    lg = RW_N.bit_length() - 1
    seg = jnp.where((r >> lg) == (c >> lg), 1.0, 0.0).astype(BF16)
    return jnp.concatenate([seg, seg], axis=0)


def _head_sum(x, seg):
    hi = x.astype(BF16)
    lo = (x - hi.astype(F32)).astype(BF16)
    return jnp.dot(jnp.concatenate([hi, lo], axis=1), seg, preferred_element_type=F32)


PREV_ROWS = 16


def _shift_mix(cur_ref, prev_ref, mu, first):
    z = cur_ref[...].astype(F32)
    last = prev_ref[PREV_ROWS - 1:PREV_ROWS, :].astype(F32)
    last = jnp.where(first, jnp.zeros_like(last), last)
    rolled = pltpu.roll(z, shift=1, axis=0)
    rid = lax.broadcasted_iota(jnp.int32, z.shape, 0)
    zp = jnp.where(rid == 0, jnp.broadcast_to(last, z.shape), rolled)
    return z + (zp - z) * mu


def _rw_prep_kernel(has_vres, seq, *refs):
    if has_vres:
        (r_ref, rp_ref, k_ref, kp_ref, v_ref, vp_ref, wa_ref, wap_ref, g_ref, gp_ref, zv_ref, zvp_ref, vf_ref,
         mu_r, mu_k, mu_v, mu_wa, mu_g, mu_zv, w0, w_up, a0, a_up, g_up, v0, v_up, k_k, k_a,
         ro, ldo, ko, vo, kko, ao, go) = refs
    else:
        (r_ref, rp_ref, k_ref, kp_ref, v_ref, vp_ref, wa_ref, wap_ref, g_ref, gp_ref,
         mu_r, mu_k, mu_v, mu_wa, mu_g, w0, w_up, a0, a_up, g_up, k_k, k_a,
         ro, ldo, ko, vo, kko, ao, go) = refs
    tm = r_ref.shape[0]
    first = (pl.program_id(0) * tm) % seq == 0

    r = _shift_mix(r_ref, rp_ref, mu_r[...], first)
    k = _shift_mix(k_ref, kp_ref, mu_k[...], first)
    v = _shift_mix(v_ref, vp_ref, mu_v[...], first)
    wa = _shift_mix(wa_ref, wap_ref, mu_wa[...], first)
    zg = _shift_mix(g_ref, gp_ref, mu_g[...], first)

    zw = w0[...] + _dot(jnp.tanh(wa), w_up[...])
    w_log = -(jnp.maximum(-zw, 0.0) + jnp.log(1.0 + jnp.exp(-jnp.abs(zw)))) - 0.5
    ldo[...] = -jnp.exp(w_log)
    a = _sigmoid(a0[...] + _dot(wa, a_up[...]))
    go[...] = _dot(_sigmoid(zg), g_up[...]).astype(go.dtype)
    if has_vres:
        zv = _shift_mix(zv_ref, zvp_ref, mu_zv[...], first)
        v = v + (vf_ref[...].astype(F32) - v) * _sigmoid(v0[...] + _dot(zv, v_up[...]))
    kk = k * k_k[...]
    seg = _head_seg(LANES)
    for p in range(RW_WIDTH // LANES):
        sl = slice(p * LANES, (p + 1) * LANES)
        kkp = kk[:, sl]
        ss = _head_sum(kkp * kkp, seg)
        kko[:, sl] = (kkp * lax.rsqrt(jnp.maximum(ss, 1e-24))).astype(kko.dtype)
    ro[...] = r.astype(ro.dtype)
    ko[...] = (k * (1.0 + (a - 1.0) * k_a[...])).astype(ko.dtype)
    vo[...] = v.astype(vo.dtype)
    ao[...] = a.astype(ao.dtype)


def rw_prep(z_a, z_b, v_first, p, seq):
    T = z_a.shape[0]
    has_vres = v_first is not None
    W = RW_WIDTH
    tm = RW_PREP_ROWS
    tb = tm // PREV_ROWS

    def cur(width, off):
        return pl.BlockSpec((tm, width), lambda i: (i, off // width))

    def prev(width, off):
        return pl.BlockSpec((PREV_ROWS, width), lambda i: (jnp.maximum(i * tb - 1, 0), off // width))

    def full(a):
        return pl.BlockSpec(a.shape, lambda i: (0,) * a.ndim)

    acts = [z_a, z_a, z_a, z_a, z_a, z_a, z_b, z_b, z_b, z_b]
    specs = [cur(W, ZA_RKV), prev(W, ZA_RKV), cur(W, ZA_RKV + W), prev(W, ZA_RKV + W),
             cur(W, ZA_RKV + 2 * W), prev(W, ZA_RKV + 2 * W),
             cur(LANES, ZB_WA), prev(LANES, ZB_WA), cur(LANES, ZB_G), prev(LANES, ZB_G)]
    if has_vres:
        acts += [z_b, z_b, v_first]
        specs += [cur(LANES, ZB_VRES), prev(LANES, ZB_VRES), pl.BlockSpec((tm, W), lambda i: (i, 0))]
        names = ["mu_r", "mu_k", "mu_v", "mu_wa", "mu_g", "mu_zv", "w0", "w_up", "a0", "a_up", "g_up", "v0", "v_up",
                 "k_k", "k_a"]
    else:
        names = ["mu_r", "mu_k", "mu_v", "mu_wa", "mu_g", "w0", "w_up", "a0", "a_up", "g_up", "k_k", "k_a"]
    params = [p[n] for n in names]
    out_spec = pl.BlockSpec((tm, W), lambda i: (i, 0))
    return pl.pallas_call(
        functools.partial(_rw_prep_kernel, has_vres, seq),
        grid=(T // tm,),
        in_specs=specs + [full(a) for a in params],
        out_specs=[out_spec] * 7,
        out_shape=[jax.ShapeDtypeStruct((T, W), F32 if i == 1 else BF16) for i in range(7)],
        compiler_params=_cparams("parallel"),
        name="rw_prep",
    )(*acts, *params)


HG_HEADS_PER_PASS = 4
RW_STAGES_PER_STEP = 1
HG_STAGES_PER_STEP = 1
RW_CHUNK = 64


def _rw_chunk_stages(r_ref, ld_ref, k_ref, v_ref, kk_ref, a_ref, g_ref, rk_ref, lg_ref, lb_ref, o_ref, st_ref, row0,
                     pairs):
    C = RW_CHUNK
    C2 = 2 * C
    rows = slice(row0, row0 + C)

    def ps(ref, p):
        return ref[rows if ref.shape[0] > 1 else slice(None), p * LANES:(p + 1) * LANES].astype(F32)

    tril = _tril_ones(C)
    seg = _head_seg(LANES)
    lane = lax.broadcasted_iota(jnp.int32, (C, LANES), 1)
    row = lax.broadcasted_iota(jnp.int32, (C2, C2), 0)
    col = lax.broadcasted_iota(jnp.int32, (C2, C2), 1)
    strict = (col & (C - 1)) < (row & (C - 1))
    incl = (col & (C - 1)) <= (row & (C - 1))
    eye = jnp.where(row == col, 1.0, 0.0)

    def stack(x):
        return jnp.concatenate([jnp.where(lane < RW_N, x, 0.0), jnp.where(lane >= RW_N, x, 0.0)], axis=0).astype(BF16)

    r = {p: ps(r_ref, p) for p in pairs}
    ld = {p: ps(ld_ref, p) for p in pairs}
    k = {p: ps(k_ref, p) for p in pairs}
    v = {p: ps(v_ref, p) for p in pairs}
    kk = {p: ps(kk_ref, p) for p in pairs}
    kka = {p: kk[p] * ps(a_ref, p) for p in pairs}
    cl_all = _cumsum_rows(tril, ld_ref[rows, pairs[0] * LANES:(pairs[-1] + 1) * LANES])
    cl = {p: cl_all[:, n * LANES:(n + 1) * LANES] for n, p in enumerate(pairs)}
    gc = {p: cl[p][C - 1:C, :] for p in pairs}
    e_inv = {p: jnp.exp(-cl[p]) for p in pairs}
    e_end = {p: jnp.exp(gc[p] - cl[p]) for p in pairs}
    ar = {p: jnp.concatenate([stack(-kk[p] * jnp.exp(cl[p] - ld[p])), stack(r[p] * jnp.exp(cl[p]))], axis=0)
          for p in pairs}
    bk = {p: jnp.concatenate([stack(kka[p] * e_inv[p]), stack(k[p] * e_inv[p])], axis=0) for p in pairs}
    bkg = {p: jnp.concatenate([stack(kka[p] * e_end[p]), stack(k[p] * e_end[p])], axis=0) for p in pairs}
    vv = {p: stack(v[p]) for p in pairs}
    yield

    g4 = {p: _dot_nt(ar[p], bk[p]) for p in pairs}
    lab = {p: jnp.where(strict, g4[p][:C2, :C2], 0.0) for p in pairs}
    lkm = {p: jnp.concatenate([jnp.where(strict, g4[p][:C2, C2:], 0.0), jnp.where(incl, g4[p][C2:, C2:], 0.0)],
                              axis=0).astype(BF16) for p in pairs}
    mb = {p: jnp.where(incl, g4[p][C2:, :C2], 0.0).astype(BF16) for p in pairs}
    yield
    qi = {p: eye + lab[p] for p in pairs}
    mp = {p: _dot(lab[p], lab[p]) for p in pairs}
    yield
    n = 2
    while 2 * n < C:
        mq = {p: _dot(mp[p], jnp.concatenate([mp[p].astype(BF16), qi[p].astype(BF16)], axis=1)) for p in pairs}
        mp = {p: mq[p][:, :C2] for p in pairs}
        qi = {p: qi[p] + mq[p][:, C2:] for p in pairs}
        yield
        n *= 2
    qi = {p: qi[p] + _dot(mp[p], qi[p]) for p in pairs}
    yield

    zt = {p: st_ref[p] for p in pairs}
    fz = {p: _dot(jnp.concatenate([ar[p], lkm[p]], axis=1), jnp.concatenate([zt[p].T.astype(BF16), vv[p]], axis=0))
          for p in pairs}
    yield
    uu = {p: _dot(qi[p], fz[p][:C2]) for p in pairs}
    yield
    yy = {p: fz[p][C2:] + _dot(mb[p], uu[p]) for p in pairs}
    for p in pairs:
        uv = jnp.concatenate([uu[p].astype(BF16), vv[p]], axis=0)
        st_ref[p] = zt[p] * jnp.exp(gc[p]) + _dot_tn(uv, bkg[p])
    yield

    y = {p: yy[p][:C] + yy[p][C:] for p in pairs}
    mean = {p: _head_sum(y[p], seg) * (1.0 / RW_N) for p in pairs}
    d = {p: y[p] - mean[p] for p in pairs}
    var = {p: _head_sum(d[p] * d[p], seg) * (1.0 / RW_N) for p in pairs}
    bonus = {p: _head_sum(r[p] * k[p] * ps(rk_ref, p), seg) * v[p] for p in pairs}
    yield
    for p in pairs:
        yn = d[p] * lax.rsqrt(var[p] + RW_LNX_EPS) * ps(lg_ref, p) + ps(lb_ref, p)
        o_ref[rows, p * LANES:(p + 1) * LANES] = ((yn + bonus[p]) * ps(g_ref, p)).astype(o_ref.dtype)


def _advance(gen, n):
    end = object()
    while True:
        for _ in range(n):
            if next(gen, end) is end:
                return
        yield


def _recurrent_kernel(hq, hf, hv, hog, hlb, hgn, hcode, r, ld, k, v, kk, a, g, rk, lg, lb, o_hg, o_rw, hst, rst):
    @pl.when(pl.program_id(1) == 0)
    def _():
        hst[...] = jnp.zeros_like(hst)
        rst[...] = jnp.zeros_like(rst)

    rw_refs = (r, ld, k, v, kk, a, g, rk, lg, lb, o_rw, rst)
    pairs = range(r.shape[1] // LANES)
    rw = itertools.chain(*[_rw_chunk_stages(*rw_refs, row0, pairs) for row0 in range(0, HG_CHUNK, RW_CHUNK)])
    n = HG_HEADS_PER_PASS
    hg = itertools.chain(*[_hgrn_stages(hq, hf, hv, hog, hlb, hgn, hcode, o_hg, hst, range(h0, h0 + n))
                           for h0 in range(0, HG_HEADS, n)])
    for _ in itertools.zip_longest(_advance(rw, RW_STAGES_PER_STEP), _advance(hg, HG_STAGES_PER_STEP)):
        pass


def recurrent_mixers(z_a, z_b, lower_bound, onorm_g, r, ld, k, v, kk, a, g, r_k, lnx_g, lnx_b, batch, seq):
    T, W = r.shape
    C = HG_CHUNK
    nc = seq // C
    codes = jnp.asarray(_hgrn_term_codes(C))

    def zspec(seg):
        return pl.BlockSpec((C, W), lambda b, c, seg=seg: (b * nc + c, ZA_HG // W + seg))

    act = pl.BlockSpec((C, W), lambda b, c: (b * nc + c, 0))
    par = pl.BlockSpec((1, W), lambda b, c: (0, 0))
    return pl.pallas_call(
        _recurrent_kernel,
        grid=(batch, nc),
        in_specs=[
            zspec(0), pl.BlockSpec((C, W), lambda b, c: (b * nc + c, ZB_F // W)), zspec(1), zspec(2),
            par,
            pl.BlockSpec((1, HG_DK), lambda b, c: (0, 0)),
            pl.BlockSpec((C, C), lambda b, c: (0, 0)),
        ] + [act] * 7 + [par] * 3,
        out_specs=[act, act],
        out_shape=[jax.ShapeDtypeStruct((T, W), BF16)] * 2,
        scratch_shapes=[pltpu.VMEM((HG_HEADS, HG_DK, HG_DK), F32), pltpu.VMEM((W // LANES, LANES, LANES), F32)],
        compiler_params=_cparams("parallel", "arbitrary"),
        name="recurrent_mixers",
    )(z_a, z_b, z_a, z_a, lower_bound, onorm_g, codes, r, ld, k, v, kk, a, g, r_k, lnx_g, lnx_b)


MLA_QK = 2 * LANES
MLA_TQ = 512
MLA_HEADS_PER_STEP = 4
MLA_VT_ROWS = MLA_V + 16


def _mla_up_kernel(cq_ref, ckv_ref, kr_ref, cs_ref, gq_ref, gkv_ref, wq_ref, wkv_ref, q_ref, kn_ref, vt_ref, krd_ref):
    scale = (MLA_NOPE + MLA_ROPE) ** -0.5 * LOG2E
    cs = cs_ref[...]
    q = _dot(_rms(cq_ref[...], gq_ref[...]), wq_ref[...])
    for h in range(MLA_HEADS):
        o = h * MLA_QK
        q_ref[:, o:o + LANES] = (q[:, o:o + LANES] * scale).astype(q_ref.dtype)
        q_ref[:, o + LANES:o + 2 * LANES] = (q[:, o + LANES:o + 2 * LANES] * (cs * scale)).astype(q_ref.dtype)
    kv = _dot(_rms(ckv_ref[...], gkv_ref[...]), wkv_ref[...])
    for h in range(MLA_HEADS):
        o = h * (MLA_NOPE + MLA_V)
        kn_ref[:, h * MLA_NOPE:(h + 1) * MLA_NOPE] = kv[:, o:o + MLA_NOPE].astype(kn_ref.dtype)
        vt_ref[h, 0, :MLA_V, :] = kv[:, o + MLA_NOPE:o + MLA_NOPE + MLA_V].T.astype(vt_ref.dtype)
        vt_ref[h, 0, MLA_V:, :] = jnp.ones((MLA_VT_ROWS - MLA_V, kv.shape[0]), vt_ref.dtype)
    krx = kr_ref[...] * cs
    krd_ref[...] = (krx + pltpu.roll(krx, shift=MLA_ROPE, axis=1)).astype(krd_ref.dtype)


def mla_up(z_b, cs, gq, gkv, wq, wkv):
    T = z_b.shape[0]
    tm = MLA_TQ
    nq = wq.shape[1]

    def full(a):
        return pl.BlockSpec(a.shape, lambda i: (0,) * a.ndim)

    return pl.pallas_call(
        _mla_up_kernel,
        grid=(T // tm,),
        in_specs=[
            pl.BlockSpec((tm, MLA_Q_RANK), lambda i: (i, ZB_CQ // MLA_Q_RANK)),
            pl.BlockSpec((tm, MLA_KV_RANK), lambda i: (i, ZB_CKV // MLA_KV_RANK)),
            pl.BlockSpec((tm, LANES), lambda i: (i, ZB_KR // LANES)),
            pl.BlockSpec((tm, LANES), lambda i: (i, 0)),
            full(gq), full(gkv), full(wq), full(wkv),
        ],
        out_specs=[
            pl.BlockSpec((tm, nq), lambda i: (i, 0)),
            pl.BlockSpec((tm, MLA_HEADS * MLA_NOPE), lambda i: (i, 0)),
            pl.BlockSpec((MLA_HEADS, 1, MLA_VT_ROWS, tm), lambda i: (0, i, 0, 0)),
            pl.BlockSpec((tm, LANES), lambda i: (i, 0)),
        ],
        out_shape=[
            jax.ShapeDtypeStruct((T, nq), BF16),
            jax.ShapeDtypeStruct((T, MLA_HEADS * MLA_NOPE), BF16),
            jax.ShapeDtypeStruct((MLA_HEADS, T // tm, MLA_VT_ROWS, tm), BF16),
            jax.ShapeDtypeStruct((T, LANES), BF16),
        ],
        compiler_params=_cparams("parallel"),
        name="mla_up",
    )(z_b, z_b, z_b, cs, gq, gkv, wq, wkv)


def _mla_attn_kernel(q_ref, qn_ref, kn_ref, kr_ref, vt_ref, o_ref, m_ref, acc_ref, s_ref):
    tq = q_ref.shape[0]
    heads = range(q_ref.shape[1] // MLA_QK)
    qi = pl.program_id(2)
    last_q = pl.num_programs(2) - 1
    q = [q_ref[:, h * MLA_QK:(h + 1) * MLA_QK] for h in heads]
    m_ref[...] = jnp.full_like(m_ref, -jnp.inf)
    acc_ref[...] = jnp.zeros_like(acc_ref)
    row = lax.broadcasted_iota(jnp.int32, (tq, tq), 0)
    col = lax.broadcasted_iota(jnp.int32, (tq, tq), 1)
    causal = row <= col

    def scores(kb, qs=q):
        ks = pl.multiple_of(kb * tq, tq)
        kr = kr_ref[pl.ds(ks, tq), :]
        kc = [jnp.concatenate([kn_ref[pl.ds(ks, tq), h * MLA_NOPE:(h + 1) * MLA_NOPE], kr], axis=1) for h in heads]
        return [lax.dot_general(kc[h], qs[h], (((1,), (1,)), ((), ())), preferred_element_type=F32)
                for h in heads]

    def update(kb):
        s = [s_ref[h] for h in heads]
        m_old = [m_ref[h] for h in heads]
        m_new = [jnp.maximum(m_old[h], jnp.max(s[h], axis=0, keepdims=True)) for h in heads]
        p = [jnp.exp2(s[h] - m_new[h]).astype(BF16) for h in heads]
        for h in heads:
            acc_ref[h] = (jnp.exp2(m_old[h] - m_new[h]) * acc_ref[h]
                          + jnp.dot(vt_ref[h, kb], p[h], preferred_element_type=F32))
            m_ref[h] = m_new[h]

    @pl.when(qi == 0)
    def _():
        s0 = scores(0)
        for h in heads:
            s_ref[h] = jnp.where(causal, s0[h], MASK_VALUE)

    def body(kb, carry):
        s_next = scores(kb + 1)
        update(kb)
        for h in heads:
            s_ref[h] = s_next[h]
        return carry

    lax.fori_loop(0, qi - 1, body, 0)

    @pl.when(qi > 0)
    def _():
        s_next = scores(qi)
        update(qi - 1)
        for h in heads:
            s_ref[h] = jnp.where(causal, s_next[h], MASK_VALUE)

    @pl.when(qi < last_q)
    def _():
        s_next = scores(0, [qn_ref[:, h * MLA_QK:(h + 1) * MLA_QK] for h in heads])
        update(qi)
        for h in heads:
            s_ref[h] = s_next[h]

    @pl.when(qi == last_q)
    def _():
        update(qi)

    for h in heads:
        acc = acc_ref[h]
        o_ref[:, h * MLA_V:(h + 1) * MLA_V] = (acc[:MLA_V] / acc[MLA_V:MLA_V + 1]).T.astype(o_ref.dtype)


def mla_attn(q, kn, vt, krd, batch, seq):
    T = q.shape[0]
    tq = MLA_TQ
    nq = seq // tq
    g = MLA_HEADS_PER_STEP
    return pl.pallas_call(
        _mla_attn_kernel,
        grid=(batch, MLA_HEADS // g, nq),
        in_specs=[
            pl.BlockSpec((tq, g * MLA_QK), lambda b, h, i: (b * nq + i, h)),
            pl.BlockSpec((tq, g * MLA_QK), lambda b, h, i: (b * nq + jnp.minimum(i + 1, nq - 1), h)),
            pl.BlockSpec((seq, g * MLA_NOPE), lambda b, h, i: (b, h)),
            pl.BlockSpec((seq, LANES), lambda b, h, i: (b, 0)),
            pl.BlockSpec((g, nq, MLA_VT_ROWS, tq), lambda b, h, i: (h, b, 0, 0)),
        ],
        out_specs=pl.BlockSpec((tq, g * MLA_V), lambda b, h, i: (b * nq + i, h)),
        out_shape=jax.ShapeDtypeStruct((T, MLA_HEADS * MLA_V), BF16),
        scratch_shapes=[pltpu.VMEM((g, 1, tq), F32), pltpu.VMEM((g, MLA_VT_ROWS, tq), F32),
                        pltpu.VMEM((g, tq, tq), F32)],
        compiler_params=_cparams("parallel", "parallel", "arbitrary"),
        name="mla_attn",
    )(q, q, kn, krd, vt)


def _merge_kernel(x_ref, ohg_ref, orw_ref, omla_ref, g0_ref, g1_ref, g2_ref, wb_ref, wo_ref, gp_ref, o_ref):
    merged = (_sigmoid(g0_ref[...].astype(F32)) * jnp.dot(ohg_ref[...], wb_ref[0], preferred_element_type=F32)
              + _sigmoid(g1_ref[...].astype(F32)) * jnp.dot(orw_ref[...], wb_ref[1], preferred_element_type=F32)
              + _sigmoid(g2_ref[...].astype(F32)) * jnp.dot(omla_ref[...], wb_ref[2], preferred_element_type=F32))
    y = _dot(merged, wo_ref[...])
    o_ref[...] = x_ref[...] + _rms(y, gp_ref[...])


def merge(x, o_hg, o_rw, o_mla, z_a, w_branch, w_out, g_post):
    T, D = x.shape
    tm = MERGE_ROWS
    row = pl.BlockSpec((tm, D), lambda i: (i, 0))

    def gate(n):
        return pl.BlockSpec((tm, D), lambda i, n=n: (i, ZA_GATE // D + n))

    def full(a):
        return pl.BlockSpec(a.shape, lambda i: (0,) * a.ndim)

    return pl.pallas_call(
        _merge_kernel,
        grid=(T // tm,),
        in_specs=[row, row, row, row, gate(0), gate(1), gate(2), full(w_branch), full(w_out), full(g_post)],
        out_specs=row,
        out_shape=jax.ShapeDtypeStruct((T, D), F32),
        compiler_params=_cparams("parallel"),
        name="merge",
    )(x, o_hg, o_rw, o_mla, z_a, z_a, z_a, w_branch, w_out, g_post)


FFN_TILE = 256


def _ffn_kernel(x_ref, gpre_ref, wg_ref, wu_ref, wo_ref, gpost_ref, o_ref, h_ref, acc_ref):
    j = pl.program_id(1)

    @pl.when(j == 0)
    def _():
        h_ref[...] = _rms(x_ref[...], gpre_ref[...]).astype(BF16)
        acc_ref[...] = jnp.zeros_like(acc_ref)

    h = h_ref[...]
    gate = jnp.dot(h, wg_ref[...], preferred_element_type=F32)
    up = jnp.dot(h, wu_ref[...], preferred_element_type=F32)
    act = gate * _sigmoid(gate) * up
    acc_ref[...] += _dot(act, wo_ref[...])

    @pl.when(j == pl.num_programs(1) - 1)
    def _():
        o_ref[...] = x_ref[...] + _rms(acc_ref[...], gpost_ref[...])


def ffn(x, g_pre, w_in, w_out, g_post):
    T, D = x.shape
    tm = FFN_ROWS
    tf = FFN_TILE
    nf = D_FF // tf
    return pl.pallas_call(
        _ffn_kernel,
        grid=(T // tm, nf),
        in_specs=[
            pl.BlockSpec((tm, D), lambda i, j: (i, 0)),
            pl.BlockSpec((1, D), lambda i, j: (0, 0)),
            pl.BlockSpec((D, tf), lambda i, j: (0, j)),
            pl.BlockSpec((D, tf), lambda i, j: (0, nf + j)),
            pl.BlockSpec((tf, D), lambda i, j: (j, 0)),
            pl.BlockSpec((1, D), lambda i, j: (0, 0)),
        ],
        out_specs=pl.BlockSpec((tm, D), lambda i, j: (i, 0)),
        out_shape=jax.ShapeDtypeStruct((T, D), F32),
        scratch_shapes=[pltpu.VMEM((tm, D), BF16), pltpu.VMEM((tm, D), F32)],
        compiler_params=_cparams("parallel", "arbitrary"),
        name="ffn",
    )(x, g_pre, w_in, w_in, w_out, g_post)


def _swap_halves(w):
    h = w.shape[-1] // 2
    return jnp.concatenate([w[..., h:], w[..., :h]], axis=-1)


def _pad_rows(w, rows, at):
    out = jnp.zeros((rows, w.shape[1]), w.dtype)
    return out.at[at:at + w.shape[0]].set(w)


def _pack_in_proj(w, w_vres):
    hg, rkv, wa, g, cq, ckv, kr, gates = jnp.split(
        w, np.cumsum([4096, 3072, RW_LORA_W + RW_LORA_A, RW_LORA_G, MLA_Q_RANK, MLA_KV_RANK, MLA_ROPE]).tolist(),
        axis=1)
    hq, hf, hi, hog = jnp.split(hg, 4, axis=1)
    wide = jnp.concatenate([hq, hi, hog, rkv, gates], axis=1)
    parts = [hf, ckv, kr, _swap_halves(kr), wa, cq, g]
    if w_vres is not None:
        parts.append(jnp.pad(w_vres, ((0, 0), (0, LANES - RW_LORA_V))))
    narrow = jnp.concatenate(parts, axis=1)
    return wide.astype(BF16), narrow.astype(BF16)


def _pack_wq(w_uq):
    w = w_uq.reshape(MLA_Q_RANK, MLA_HEADS, MLA_NOPE + MLA_ROPE)
    rope = w[..., MLA_NOPE:]
    w = jnp.concatenate([w[..., :MLA_NOPE], rope, _swap_halves(rope)], axis=-1)
    return w.reshape(MLA_Q_RANK, MLA_HEADS * MLA_QK).astype(BF16)


def _row(v):
    return v.reshape(1, -1).astype(F32)


def kernel(x, positions, hgrn_lb_logits, mix_pre_g, mix_post_g, ffn_pre_g, ffn_post_g, w_in, w_vres_down,
           hgrn_onorm_g, rwkv_mu, rwkv_vres_mu, rwkv_w0, rwkv_w_up, rwkv_a0, rwkv_a_up, rwkv_g_up, rwkv_v0,
           rwkv_v_up, rwkv_k_k, rwkv_k_a, rwkv_r_k, rwkv_lnx_g, rwkv_lnx_b, mla_q_norm_g, mla_w_uq,
           mla_kv_norm_g, mla_w_ukv, w_branch, w_out, w_ffn_in, w_ffn_out):
    batch, seq, d = x.shape
    depth = w_in.shape[0]
    T = batch * seq
    xt = x.reshape(T, d)

    probs = jax.nn.softmax(hgrn_lb_logits.astype(F32), axis=0)
    lower_bounds = jnp.cumsum(probs, axis=0) - probs[0]
    inv_freq = ROPE_THETA ** (-jnp.arange(0, MLA_ROPE, 2, dtype=F32) / MLA_ROPE)
    ang = positions.astype(F32).reshape(T, 1) * inv_freq
    cos, sin = jnp.cos(ang), jnp.sin(ang)
    cs = jnp.concatenate([cos, cos, -sin, sin], axis=-1)

    W = RW_WIDTH
    v_first = None
    for l in range(depth):
        w_wide, w_narrow = _pack_in_proj(w_in[l], None if l == 0 else w_vres_down[l - 1])
        g_pre = _row(mix_pre_g[l])
        z_a = norm_matmul(xt, g_pre, w_wide, PROJ_COLS, BF16)
        z_b = norm_matmul(xt, g_pre, w_narrow, w_narrow.shape[1], F32)

        mu = rwkv_mu[l]
        p = {
            "mu_r": _row(mu[:W]), "mu_k": _row(mu[W:2 * W]), "mu_v": _row(mu[2 * W:3 * W]),
            "mu_wa": _row(mu[3 * W:3 * W + LANES]), "mu_g": _row(mu[3 * W + LANES:]),
            "w0": _row(rwkv_w0[l]), "w_up": _pad_rows(rwkv_w_up[l], LANES, 0).astype(BF16),
            "a0": _row(rwkv_a0[l]), "a_up": _pad_rows(rwkv_a_up[l], LANES, RW_LORA_W).astype(BF16),
            "g_up": rwkv_g_up[l].astype(BF16),
            "k_k": _row(rwkv_k_k[l]), "k_a": _row(rwkv_k_a[l]),
        }
        if l > 0:
            p["mu_zv"] = _row(jnp.pad(rwkv_vres_mu[l - 1], (0, LANES - RW_LORA_V)))
            p["v0"] = _row(rwkv_v0[l - 1])
            p["v_up"] = _pad_rows(rwkv_v_up[l - 1], LANES, 0).astype(BF16)
        r, ld, k, v, kk, a, g = rw_prep(z_a, z_b, v_first, p, seq)
        if l == 0:
            v_first = v
        o_hg, o_rw = recurrent_mixers(z_a, z_b, _row(lower_bounds[l]), _row(hgrn_onorm_g[l]), r, ld, k, v, kk, a, g,
                                      _row(rwkv_r_k[l]), _row(rwkv_lnx_g[l]), _row(rwkv_lnx_b[l]), batch, seq)

        q, kn, vt, krd = mla_up(z_b, cs, _row(mla_q_norm_g[l]), _row(mla_kv_norm_g[l]), _pack_wq(mla_w_uq[l]),
                                mla_w_ukv[l].astype(BF16))
        o_mla = mla_attn(q, kn, vt, krd, batch, seq)

        xt = merge(xt, o_hg, o_rw, o_mla, z_a, w_branch[l].astype(BF16), w_out[l].astype(BF16),
                   _row(mix_post_g[l]))
        xt = ffn(xt, _row(ffn_pre_g[l]), w_ffn_in[l].astype(BF16), w_ffn_out[l].astype(BF16),
                 _row(ffn_post_g[l]))
    return xt.reshape(batch, seq, d)
```
